```python
import math
import jax, jax.numpy as jnp
from jax import lax
import numpy as np

D_MODEL = 4096
BATCH = 1
SEQ = 8192
DEPTH = 2

F32 = jnp.float32
NORM_EPS = 1e-6

ATT_GROUPS = ((128, 1), (512, 4), (2048, 16))
ATT_N_GROUPS = len(ATT_GROUPS)
ATT_HEADS_PER_GROUP = D_MODEL // 512
ATT_HEAD_DIM = 128
ATT_WIDTH = ATT_N_GROUPS * ATT_HEADS_PER_GROUP * ATT_HEAD_DIM
ATT_OUT = ATT_HEADS_PER_GROUP * ATT_HEAD_DIM
ATT_BLOCK = 128
ATT_COLS = 3 * ATT_WIDTH

RWKV_HEAD_SIZE = 64
RWKV_WIDTH = 3 * D_MODEL // 4
RWKV_HEADS = RWKV_WIDTH // RWKV_HEAD_SIZE
RWKV_DECAY_LORA = max(32, int(round(1.8 * D_MODEL ** 0.5 / 32)) * 32)
RWKV_AAA_LORA = max(32, int(round(1.8 * D_MODEL ** 0.5 / 32)) * 32)
RWKV_GATE_LORA = max(32, int(round(0.6 * D_MODEL ** 0.8 / 32)) * 32)
RWKV_COLS = 3 * RWKV_WIDTH + RWKV_DECAY_LORA + RWKV_AAA_LORA + RWKV_GATE_LORA
RWKV_LN_EPS = 64e-5

SSM_INNER = 3 * D_MODEL // 4
SSM_HEAD_DIM = 64
SSM_HEADS = SSM_INNER // SSM_HEAD_DIM
SSM_GROUPS = 8
SSM_STATE = 128
SSM_CONV = 4
SSM_CHUNK = 128
SSM_CONV_DIM = SSM_INNER + 2 * SSM_GROUPS * SSM_STATE
SSM_COLS = SSM_INNER + SSM_CONV_DIM + SSM_HEADS
SSM_NORM_EPS = 1e-5

N_BRANCHES = 3
GATE_COLS = N_BRANCHES * D_MODEL
IN_COLS = ATT_COLS + RWKV_COLS + SSM_COLS + GATE_COLS
FFN_HIDDEN = -(-8 * D_MODEL // (3 * 256)) * 256

kernel_name = "hybrid_dilated_attn_rwkv7_mamba2_gated"


def rms_norm(x, g):
    xf = x.astype(F32)
    y = xf * lax.rsqrt(jnp.mean(xf * xf, axis=-1, keepdims=True) + NORM_EPS)
    return (y * g.astype(F32)).astype(x.dtype)


def dilated_window_attention(q, k, v, window, dilation):
    B_, T, H, E = q.shape
    n_back = window // dilation
    span = dilation * ATT_BLOCK
    Lp = -(-T // span) * span
    nb = Lp // span
    pad = ((0, 0), (0, Lp - T), (0, 0), (0, 0))

    def to_blocks(t):
        t = jnp.pad(t, pad).reshape(B_, Lp // dilation, dilation, H, E)
        return jnp.swapaxes(t, 1, 2).reshape(B_, dilation, nb, ATT_BLOCK, H, E)

    def with_prev(t):
        prev = jnp.pad(t[:, :, :-1], ((0, 0), (0, 0), (1, 0), (0, 0), (0, 0), (0, 0)))
        return jnp.concatenate([prev, t], axis=3)

    def from_blocks(t):
        rest = t.shape[4:]
        t = t.reshape((B_, dilation, Lp // dilation) + rest)
        return jnp.swapaxes(t, 1, 2).reshape((B_, Lp) + rest)[:, :T]

    qb = to_blocks(q).astype(F32)
    kw = with_prev(to_blocks(k)).astype(F32)
    vw = with_prev(to_blocks(v)).astype(F32)
    s = jnp.einsum("bdnqhe,bdnkhe->bdnhqk", qb, kw) * (ATT_HEAD_DIM ** -0.5)
    qi = jnp.arange(ATT_BLOCK)[:, None]
    kj = jnp.arange(2 * ATT_BLOCK)[None, :]
    dist = qi + ATT_BLOCK - kj
    band = (dist >= 0) & (dist <= n_back)
    blk = jnp.arange(nb)[:, None, None]
    mask = band[None] & ((blk > 0) | (kj[None] >= ATT_BLOCK))
    s = jnp.where(mask[None, None, :, None], s, -jnp.inf)
    m = jnp.max(s, axis=-1, keepdims=True)
    p = jnp.exp(s - m)
    den = jnp.sum(p, axis=-1, keepdims=True)
    o = jnp.einsum("bdnhqk,bdnkhe->bdnqhe", p / den, vw)
    lse = jnp.swapaxes((m + jnp.log(den))[..., 0], -1, -2)
    return from_blocks(o), from_blocks(lse)


def dilated_attention_mixer(qkv):
    B_, T, _ = qkv.shape
    q, k, v = jnp.split(qkv, 3, axis=-1)
    shp = (B_, T, ATT_N_GROUPS, ATT_HEADS_PER_GROUP, ATT_HEAD_DIM)
    q, k, v = q.reshape(shp), k.reshape(shp), v.reshape(shp)
    outs, lses = [], []
    for gi, (window, dilation) in enumerate(ATT_GROUPS):
        o, lse = dilated_window_attention(q[:, :, gi], k[:, :, gi], v[:, :, gi], window, dilation)
        outs.append(o)
        lses.append(lse)
    wts = jax.nn.softmax(jnp.stack(lses), axis=0)[..., None]
    y = jnp.sum(wts * jnp.stack(outs), axis=0)
    return y.reshape(B_, T, ATT_OUT).astype(qkv.dtype)


def rwkv7_scan(r, w, k, v, a, b):
    B_, T, H, N = r.shape

    def step(S, inp):
        r_t, w_t, k_t, v_t, a_t, b_t = inp
        sa = jnp.einsum("bhij,bhj->bhi", S, a_t)
        S = S * w_t[:, :, None, :] + sa[..., None] * b_t[:, :, None, :] + v_t[..., None] * k_t[:, :, None, :]
        return S, jnp.einsum("bhij,bhj->bhi", S, r_t)

    xs = tuple(jnp.swapaxes(t, 0, 1) for t in (r, w, k, v, a, b))
    S0 = jnp.zeros((B_, H, N, N), F32)
    _, ys = lax.scan(step, S0, xs)
    return jnp.swapaxes(ys, 0, 1)


def rwkv7_mixer(u, mu, w0, w2, a0, a2, g2, k_k, k_a, r_k, ln_w, ln_b):
    B_, T, _ = u.shape
    u_prev = jnp.pad(u[:, :-1], ((0, 0), (1, 0), (0, 0)))
    u = u + (u_prev - u) * mu
    r, k, v, xw, xa, xg = jnp.split(
        u,
        [RWKV_WIDTH, 2 * RWKV_WIDTH, 3 * RWKV_WIDTH,
         3 * RWKV_WIDTH + RWKV_DECAY_LORA, 3 * RWKV_WIDTH + RWKV_DECAY_LORA + RWKV_AAA_LORA],
        axis=-1)
    w_log = -jax.nn.softplus(-(w0 + jnp.tanh(xw) @ w2)) - 0.5
    decay = jnp.exp(-jnp.exp(w_log.astype(F32)))
    a = jax.nn.sigmoid(a0 + xa @ a2)
    g = jax.nn.sigmoid(xg) @ g2

    def heads(t):
        return t.astype(F32).reshape(B_, T, RWKV_HEADS, RWKV_HEAD_SIZE)

    kk = heads(k * k_k)
    kk = kk / jnp.maximum(jnp.linalg.norm(kk, axis=-1, keepdims=True), 1e-12)
    k = k * (1 + (a - 1) * k_a)
    rh, kh, vh, ah, wh = heads(r), heads(k), heads(v), heads(a), heads(decay)
    y = rwkv7_scan(rh, wh, kh, vh, -kk, kk * ah)
    mean = jnp.mean(y, axis=-1, keepdims=True)
    var = jnp.mean((y - mean) ** 2, axis=-1, keepdims=True)
    y = ((y - mean) * lax.rsqrt(var + RWKV_LN_EPS)).reshape(B_, T, RWKV_WIDTH)
    y = y * ln_w.astype(F32) + ln_b.astype(F32)
    bonus = jnp.sum(rh * kh * r_k.astype(F32), axis=-1, keepdims=True) * vh
    y = y + bonus.reshape(B_, T, RWKV_WIDTH)
    return (y * g.astype(F32)).astype(u.dtype)


def causal_depthwise_conv(u, w, b):
    C = u.shape[-1]
    y = lax.conv_general_dilated(
        u, w[:, None, :].astype(u.dtype), window_strides=(1,), padding=[(SSM_CONV - 1, 0)],
        dimension_numbers=("NWC", "WIO", "NWC"), feature_group_count=C)
    return y + b


def ssd_chunked(X, dA, Bm, Cm):
    B_, T, H, P = X.shape
    nc = T // SSM_CHUNK
    hg = H // SSM_GROUPS
    X = X.reshape(B_, nc, SSM_CHUNK, SSM_GROUPS, hg, P)
    dA = dA.reshape(B_, nc, SSM_CHUNK, SSM_GROUPS, hg)
    Bm = Bm.reshape(B_, nc, SSM_CHUNK, SSM_GROUPS, SSM_STATE)
    Cm = Cm.reshape(B_, nc, SSM_CHUNK, SSM_GROUPS, SSM_STATE)
    a_cum = jnp.cumsum(dA, axis=2)
    a_t = jnp.moveaxis(a_cum, 2, -1)
    seg = a_t[..., :, None] - a_t[..., None, :]
    causal = jnp.tril(jnp.ones((SSM_CHUNK, SSM_CHUNK), dtype=bool))
    Lmat = jnp.exp(jnp.where(causal, seg, -jnp.inf))
    cb = jnp.einsum("bclgn,bcsgn->bcgls", Cm, Bm)
    y_diag = jnp.einsum("bcgls,bcghls,bcsghp->bclghp", cb, Lmat, X)
    decay_states = jnp.exp(a_cum[:, :, -1:] - a_cum)
    states = jnp.einsum("bclgn,bclgh,bclghp->bcghpn", Bm, decay_states, X)
    chunk_decay = jnp.exp(a_cum[:, :, -1])

    def step(s, inp):
        st, dec = inp
        return s * dec[..., None, None] + st, s

    s0 = jnp.zeros((B_, SSM_GROUPS, hg, P, SSM_STATE), X.dtype)
    _, s_in = lax.scan(step, s0, (jnp.moveaxis(states, 1, 0), jnp.moveaxis(chunk_decay, 1, 0)))
    s_in = jnp.moveaxis(s_in, 0, 1)
    y_off = jnp.einsum("bclgn,bcghpn,bclgh->bclghp", Cm, s_in, jnp.exp(a_cum))
    return (y_diag + y_off).reshape(B_, T, H, P)


def mamba2_mixer(u, conv_w, conv_b, dt_bias, a_log, d_skip, norm_g):
    B_, T, _ = u.shape
    z, xbc, dt = jnp.split(u, [SSM_INNER, SSM_INNER + SSM_CONV_DIM], axis=-1)
    xbc = jax.nn.silu(causal_depthwise_conv(xbc, conv_w, conv_b)).astype(F32)
    xs, bm, cm = jnp.split(xbc, [SSM_INNER, SSM_INNER + SSM_GROUPS * SSM_STATE], axis=-1)
    xs = xs.reshape(B_, T, SSM_HEADS, SSM_HEAD_DIM)
    bm = bm.reshape(B_, T, SSM_GROUPS, SSM_STATE)
    cm = cm.reshape(B_, T, SSM_GROUPS, SSM_STATE)
    dt = jax.nn.softplus(dt.astype(F32) + dt_bias.astype(F32))
    a = -jnp.exp(a_log.astype(F32))
    y = ssd_chunked(xs * dt[..., None], dt * a, bm, cm)
    y = y + xs * d_skip.astype(F32)[:, None]
    y = y.reshape(B_, T, SSM_INNER) * jax.nn.silu(z.astype(F32))
    yg = y.reshape(B_, T, SSM_GROUPS, SSM_INNER // SSM_GROUPS)
    yg = yg * lax.rsqrt(jnp.mean(yg * yg, axis=-1, keepdims=True) + SSM_NORM_EPS)
    return (yg.reshape(B_, T, SSM_INNER) * norm_g.astype(F32)).astype(u.dtype)


def setup_inputs(seed: int = 0) -> dict:
    key = jax.random.key(seed)
    ks = jax.random.split(key, 32)
    L = DEPTH

    def nrm(k, shape, scale):
        return scale * jax.random.normal(k, shape, F32)

    def gain(k, shape):
        return 1.0 + 0.02 * jax.random.normal(k, shape, F32)

    dt0 = jnp.exp(jax.random.uniform(ks[16], (L, SSM_HEADS), F32, math.log(1e-3), math.log(1e-1)))
    return {
        "x": nrm(ks[0], (BATCH, SEQ, D_MODEL), 1.0),
        "norm_mix_g": gain(ks[1], (L, D_MODEL)),
        "w_in": nrm(ks[2], (L, D_MODEL, IN_COLS), D_MODEL ** -0.5),
        "rwkv_mu": jax.random.uniform(ks[3], (L, RWKV_COLS), F32),
        "rwkv_w0": jax.random.uniform(ks[4], (L, RWKV_WIDTH), F32, -6.5, -1.5),
        "rwkv_w2": nrm(ks[5], (L, RWKV_DECAY_LORA, RWKV_WIDTH), 0.1 * RWKV_DECAY_LORA ** -0.5),
        "rwkv_a0": nrm(ks[6], (L, RWKV_WIDTH), 0.1),
        "rwkv_a2": nrm(ks[7], (L, RWKV_AAA_LORA, RWKV_WIDTH), 0.1 * RWKV_AAA_LORA ** -0.5),
        "rwkv_g2": nrm(ks[8], (L, RWKV_GATE_LORA, RWKV_WIDTH), RWKV_GATE_LORA ** -0.5),
        "rwkv_k_k": 0.85 + nrm(ks[9], (L, RWKV_WIDTH), 0.02),
        "rwkv_k_a": gain(ks[10], (L, RWKV_WIDTH)),
        "rwkv_r_k": nrm(ks[11], (L, RWKV_HEADS, RWKV_HEAD_SIZE), 0.1),
        "rwkv_ln_w": gain(ks[12], (L, RWKV_WIDTH)),
        "rwkv_ln_b": nrm(ks[13], (L, RWKV_WIDTH), 0.01),
        "ssm_conv_w": nrm(ks[14], (L, SSM_CONV, SSM_CONV_DIM), SSM_CONV ** -0.5),
        "ssm_conv_b": nrm(ks[15], (L, SSM_CONV_DIM), 0.01),
        "ssm_dt_bias": dt0 + jnp.log(-jnp.expm1(-dt0)),
        "ssm_a_log": jnp.log(jax.random.uniform(ks[17], (L, SSM_HEADS), F32, 1.0, 16.0)),
        "ssm_d": gain(ks[18], (L, SSM_HEADS)),
        "ssm_norm_g": gain(ks[19], (L, SSM_INNER)),
        "p_attn": nrm(ks[20], (L, ATT_OUT, D_MODEL), ATT_OUT ** -0.5),
        "p_rwkv": nrm(ks[21], (L, RWKV_WIDTH, D_MODEL), RWKV_WIDTH ** -0.5),
        "p_ssm": nrm(ks[22], (L, SSM_INNER, D_MODEL), SSM_INNER ** -0.5),
        "w_out": nrm(ks[23], (L, D_MODEL, D_MODEL), D_MODEL ** -0.5),
        "norm_ffn_g": gain(ks[24], (L, D_MODEL)),
        "w_ffn_gate": nrm(ks[25], (L, D_MODEL, FFN_HIDDEN), D_MODEL ** -0.5),
        "w_ffn_up": nrm(ks[26], (L, D_MODEL, FFN_HIDDEN), D_MODEL ** -0.5),
        "w_ffn_down": nrm(ks[27], (L, FFN_HIDDEN, D_MODEL), FFN_HIDDEN ** -0.5),
        "norm_final_g": gain(ks[28], (D_MODEL,)),
    }


def reference(x, norm_mix_g, w_in, rwkv_mu, rwkv_w0, rwkv_w2, rwkv_a0, rwkv_a2, rwkv_g2,
              rwkv_k_k, rwkv_k_a, rwkv_r_k, rwkv_ln_w, rwkv_ln_b, ssm_conv_w, ssm_conv_b,
              ssm_dt_bias, ssm_a_log, ssm_d, ssm_norm_g, p_attn, p_rwkv, p_ssm, w_out,
              norm_ffn_g, w_ffn_gate, w_ffn_up, w_ffn_down, norm_final_g):
    for l in range(DEPTH):
        h = rms_norm(x, norm_mix_g[l])
        proj = h @ w_in[l]
        att_in, rwkv_in, ssm_in, gate_in = jnp.split(
            proj, [ATT_COLS, ATT_COLS + RWKV_COLS, ATT_COLS + RWKV_COLS + SSM_COLS], axis=-1)
        y_att = dilated_attention_mixer(att_in)
        y_rwkv = rwkv7_mixer(rwkv_in, rwkv_mu[l], rwkv_w0[l], rwkv_w2[l], rwkv_a0[l], rwkv_a2[l],
                             rwkv_g2[l], rwkv_k_k[l], rwkv_k_a[l], rwkv_r_k[l], rwkv_ln_w[l], rwkv_ln_b[l])
        y_ssm = mamba2_mixer(ssm_in, ssm_conv_w[l], ssm_conv_b[l], ssm_dt_bias[l], ssm_a_log[l],
                             ssm_d[l], ssm_norm_g[l])
        g_att, g_rwkv, g_ssm = jnp.split(jax.nn.sigmoid(gate_in), N_BRANCHES, axis=-1)
        merged = g_att * (y_att @ p_attn[l]) + g_rwkv * (y_rwkv @ p_rwkv[l]) + g_ssm * (y_ssm @ p_ssm[l])
        x = x + merged @ w_out[l]
        h2 = rms_norm(x, norm_ffn_g[l])
        x = x + (jax.nn.silu(h2 @ w_ffn_gate[l]) * (h2 @ w_ffn_up[l])) @ w_ffn_down[l]
    return rms_norm(x, norm_final_g)
```

```python
import functools

import jax
import jax.numpy as jnp
from jax import lax
from jax.experimental import pallas as pl
from jax.experimental.pallas import tpu as pltpu

F32 = jnp.float32
BF16 = jnp.bfloat16

D_MODEL = 4096
DEPTH = 2
NORM_EPS = 1e-6
ATT_GROUPS = ((128, 1), (512, 4), (2048, 16))
ATT_HEADS = 8
ATT_DIM = 128
ATT_WIDTH = 3072
ATT_OUT = 1024
ATT_COLS = 9216
RWKV_WIDTH = 3072
RWKV_HEAD = 64
RWKV_LORA_W = 128
RWKV_LORA_A = 128
RWKV_LORA_G = 480
RWKV_COLS = 9952
RWKV_LN_EPS = 64e-5
SSM_INNER = 3072
SSM_HEADS = 48
SSM_GROUPS = 8
SSM_STATE = 128
SSM_CONV = 4
SSM_CHUNK = 128
SSM_CONV_DIM = 5120
SSM_COLS = 8240
SSM_NORM_EPS = 1e-5
FFN_HIDDEN = 11008

RWKV_PAD = 10240
RWKV_G_PAD = 512
SSM_PAD = 8704
FFN_PAD = 11264

LANES = 128
RWKV_CHUNK = 64
VMEM_LIMIT = 56 * 1024 * 1024


def _params(sem):
    return pltpu.CompilerParams(dimension_semantics=sem, vmem_limit_bytes=VMEM_LIMIT)


def _bdot(a, b):
    return jnp.dot(a.astype(BF16), b.astype(BF16), preferred_element_type=F32)


def _bdot_nt(a, b):
    return lax.dot_general(a.astype(BF16), b.astype(BF16), (((1,), (1,)), ((), ())),
                           preferred_element_type=F32)


def _bdot_tn(a, b):
    return lax.dot_general(a.astype(BF16), b.astype(BF16), (((0,), (0,)), ((), ())),
                           preferred_element_type=F32)


def _split2(x):
    hi = x.astype(BF16)
    lo = (x - hi.astype(F32)).astype(BF16)
    return hi, lo


def _split3(x):
    hi = x.astype(BF16)
    r1 = x - hi.astype(F32)
    mid = r1.astype(BF16)
    lo = (r1 - mid.astype(F32)).astype(BF16)
    return hi, mid, lo


def _dot_exact_lhs(m_bf16, x):
    hi, mid, lo = _split3(x)
    d = functools.partial(jnp.dot, preferred_element_type=F32)
    return d(m_bf16, hi) + d(m_bf16, mid) + d(m_bf16, lo)


def _dot_exact_rhs(x, m_bf16):
    hi, mid, lo = _split3(x)
    d = functools.partial(jnp.dot, preferred_element_type=F32)
    return d(hi, m_bf16) + d(mid, m_bf16) + d(lo, m_bf16)


def _dot_hi(a, b):
    ah, al = _split2(a)
    bh, bl = _split2(b)
    d = functools.partial(jnp.dot, preferred_element_type=F32)
    return d(ah, bh) + d(ah, bl) + d(al, bh)


def _sigmoid(x):
    return 1.0 / (1.0 + jnp.exp(-x))


def _silu(x):
    return x * _sigmoid(x)


def _softplus(x):
    return jnp.maximum(x, 0.0) + jnp.log(1.0 + jnp.exp(-jnp.abs(x)))


def _rms_kernel(x_ref, g_ref, o_ref):
    x = x_ref[...]
    ms = jnp.mean(x * x, axis=-1, keepdims=True)
    o_ref[...] = (x * lax.rsqrt(ms + NORM_EPS) * g_ref[...]).astype(o_ref.dtype)


def rms_norm(x, g, out_dtype, tr=256):
    T, D = x.shape
    return pl.pallas_call(
        _rms_kernel,
        out_shape=jax.ShapeDtypeStruct((T, D), out_dtype),
        grid=(T // tr,),
        in_specs=[pl.BlockSpec((tr, D), lambda i: (i, 0)),
                  pl.BlockSpec((1, D), lambda i: (0, 0))],
        out_specs=pl.BlockSpec((tr, D), lambda i: (i, 0)),
        compiler_params=_params(("parallel",)),
        name="rms_norm",
    )(x, g.reshape(1, D))


def _mm_kernel(a_ref, w_ref, o_ref):
    o_ref[...] = jnp.dot(a_ref[...], w_ref[...], preferred_element_type=F32).astype(o_ref.dtype)


def matmul(a, w, out_dtype, tm, tn, name):
    M, K = a.shape
    _, N = w.shape
    return pl.pallas_call(
        _mm_kernel,
        out_shape=jax.ShapeDtypeStruct((M, N), out_dtype),
        grid=(M // tm, N // tn),
        in_specs=[pl.BlockSpec((tm, K), lambda i, j: (i, 0)),
                  pl.BlockSpec((K, tn), lambda i, j: (0, j))],
        out_specs=pl.BlockSpec((tm, tn), lambda i, j: (i, j)),
        compiler_params=_params(("parallel", "arbitrary")),
        name=name,
    )(a, w)


def _mm_res_kernel(a_ref, w_ref, x_ref, o_ref):
    o_ref[...] = x_ref[...] + jnp.dot(a_ref[...], w_ref[...], preferred_element_type=F32)


def matmul_residual(a, w, x, tm, tn, name):
    M, K = a.shape
    _, N = w.shape
    return pl.pallas_call(
        _mm_res_kernel,
        out_shape=jax.ShapeDtypeStruct((M, N), F32),
        grid=(M // tm, N // tn),
        in_specs=[pl.BlockSpec((tm, K), lambda i, j: (i, 0)),
                  pl.BlockSpec((K, tn), lambda i, j: (0, j)),
                  pl.BlockSpec((tm, tn), lambda i, j: (i, j))],
        out_specs=pl.BlockSpec((tm, tn), lambda i, j: (i, j)),
        compiler_params=_params(("parallel", "arbitrary")),
        name=name,
    )(a, w, x)


def _ffn_up_kernel(h_ref, wg_ref, wu_ref, o_ref):
    h = h_ref[...]
    gate = jnp.dot(h, wg_ref[...], preferred_element_type=F32)
    up = jnp.dot(h, wu_ref[...], preferred_element_type=F32)
    o_ref[...] = (_silu(gate) * up).astype(o_ref.dtype)


def ffn_up(h, wg, wu, tm, tn):
    M, K = h.shape
    _, N = wg.shape
    return pl.pallas_call(
        _ffn_up_kernel,
        out_shape=jax.ShapeDtypeStruct((M, N), BF16),
        grid=(M // tm, N // tn),
        in_specs=[pl.BlockSpec((tm, K), lambda i, j: (i, 0)),
                  pl.BlockSpec((K, tn), lambda i, j: (0, j)),
                  pl.BlockSpec((K, tn), lambda i, j: (0, j))],
        out_specs=pl.BlockSpec((tm, tn), lambda i, j: (i, j)),
        compiler_params=_params(("parallel", "arbitrary")),
        name="ffn_up",
    )(h, wg, wu)


def _mm_acc_res_kernel(a_ref, w_ref, x_ref, o_ref, acc_ref):
    k = pl.program_id(2)

    @pl.when(k == 0)
    def _():
        acc_ref[...] = x_ref[...]

    acc_ref[...] += jnp.dot(a_ref[...], w_ref[...], preferred_element_type=F32)

    @pl.when(k == pl.num_programs(2) - 1)
    def _():
        o_ref[...] = acc_ref[...]


def matmul_residual_ksplit(a, w, x, tm, tn, tk, name):
    M, K = a.shape
    _, N = w.shape
    return pl.pallas_call(
        _mm_acc_res_kernel,
        out_shape=jax.ShapeDtypeStruct((M, N), F32),
        grid=(M // tm, N // tn, K // tk),
        in_specs=[pl.BlockSpec((tm, tk), lambda i, j, k: (i, k)),
                  pl.BlockSpec((tk, tn), lambda i, j, k: (k, j)),
                  pl.BlockSpec((tm, tn), lambda i, j, k: (i, j))],
        out_specs=pl.BlockSpec((tm, tn), lambda i, j, k: (i, j)),
        scratch_shapes=[pltpu.VMEM((tm, tn), F32)],
        compiler_params=_params(("parallel", "arbitrary", "arbitrary")),
        name=name,
    )(a, w, x)


def _merge_proj_kernel(ya_ref, yr_ref, ys_ref, pa_ref, pr_ref, ps_ref,
                       ga_ref, gr_ref, gs_ref, o_ref):
    acc = _sigmoid(ga_ref[...]) * jnp.dot(ya_ref[...], pa_ref[...], preferred_element_type=F32)
    acc += _sigmoid(gr_ref[...]) * jnp.dot(yr_ref[...], pr_ref[...], preferred_element_type=F32)
    acc += _sigmoid(gs_ref[...]) * jnp.dot(ys_ref[...], ps_ref[...], preferred_element_type=F32)
    o_ref[...] = acc.astype(o_ref.dtype)


def merge_proj(y_att, y_rwkv, y_ssm, p_att, p_rwkv, p_ssm, gates, tm, tn):
    M = y_att.shape[0]
    N = p_att.shape[1]
    nb = N // tn
    return pl.pallas_call(
        _merge_proj_kernel,
        out_shape=jax.ShapeDtypeStruct((M, N), BF16),
        grid=(M // tm, nb),
        in_specs=[pl.BlockSpec((tm, y_att.shape[1]), lambda i, j: (i, 0)),
                  pl.BlockSpec((tm, y_rwkv.shape[1]), lambda i, j: (i, 0)),
                  pl.BlockSpec((tm, y_ssm.shape[1]), lambda i, j: (i, 0)),
                  pl.BlockSpec((p_att.shape[0], tn), lambda i, j: (0, j)),
                  pl.BlockSpec((p_rwkv.shape[0], tn), lambda i, j: (0, j)),
                  pl.BlockSpec((p_ssm.shape[0], tn), lambda i, j: (0, j)),
                  pl.BlockSpec((tm, tn), lambda i, j: (i, j)),
                  pl.BlockSpec((tm, tn), lambda i, j: (i, j + nb)),
                  pl.BlockSpec((tm, tn), lambda i, j: (i, j + 2 * nb))],
        out_specs=pl.BlockSpec((tm, tn), lambda i, j: (i, j)),
        compiler_params=_params(("parallel", "arbitrary")),
        name="merge_proj",
    )(y_att, y_rwkv, y_ssm, p_att, p_rwkv, p_ssm, gates, gates, gates)


def _attn_kernel(q_ref, kp_ref, kc_ref, vp_ref, vc_ref, o_ref, lse_ref, *, n_back):
    n = pl.program_id(1)
    blk = ATT_DIM
    qi = lax.broadcasted_iota(jnp.int32, (blk, 2 * blk), 0)
    kj = lax.broadcasted_iota(jnp.int32, (blk, 2 * blk), 1)
    dist = qi + blk - kj
    mask = (dist >= 0) & (dist <= n_back) & jnp.logical_or(n > 0, kj >= blk)
    scale = ATT_DIM ** -0.5
    for h in range(ATT_HEADS):
        sl = slice(h * blk, (h + 1) * blk)
        q = q_ref[:, sl]
        k = jnp.concatenate([kp_ref[:, sl], kc_ref[:, sl]], axis=0)
        v = jnp.concatenate([vp_ref[:, sl], vc_ref[:, sl]], axis=0)
        s = lax.dot_general(q, k, (((1,), (1,)), ((), ())), preferred_element_type=F32) * scale
        s = jnp.where(mask, s, -jnp.inf)
        m = jnp.max(s, axis=-1, keepdims=True)
        p = jnp.exp(s - m)
        den = jnp.sum(p, axis=-1, keepdims=True)
        o = jnp.dot((p / den).astype(BF16), v, preferred_element_type=F32)
        o_ref[:, sl] = o
        lse_ref[:, sl] = jnp.broadcast_to(m + jnp.log(den), (blk, blk))


def dilated_attention_group(qkv, gi, window, dilation):
    T = qkv.shape[0]
    d = dilation
    nb = T // (d * ATT_DIM)
    cb = ATT_COLS // ATT_OUT
    view = qkv.reshape(T // d, d * ATT_COLS)
    qcol, kcol, vcol = gi, 3 + gi, 6 + gi
    blk = (ATT_DIM, ATT_OUT)

    def cur(col):
        return pl.BlockSpec(blk, lambda r, n: (n, r * cb + col))

    def prev(col):
        return pl.BlockSpec(blk, lambda r, n: (jnp.maximum(n - 1, 0), r * cb + col))

    out_spec = pl.BlockSpec(blk, lambda r, n: (n, r))
    o, lse = pl.pallas_call(
        functools.partial(_attn_kernel, n_back=window // dilation),
        out_shape=[jax.ShapeDtypeStruct((T // d, d * ATT_OUT), F32)] * 2,
        grid=(d, nb),
        in_specs=[cur(qcol), prev(kcol), cur(kcol), prev(vcol), cur(vcol)],
        out_specs=[out_spec, out_spec],
        compiler_params=_params(("parallel", "arbitrary")),
        name=f"attn_g{gi}",
    )(view, view, view, view, view)
    return o.reshape(T, ATT_OUT), lse.reshape(T, ATT_OUT)


def _attn_merge_kernel(o0, o1, o2, l0, l1, l2, y_ref):
    a, b, c = l0[...], l1[...], l2[...]
    m = jnp.maximum(jnp.maximum(a, b), c)
    ea, eb, ec = jnp.exp(a - m), jnp.exp(b - m), jnp.exp(c - m)
    y = (ea * o0[...] + eb * o1[...] + ec * o2[...]) / (ea + eb + ec)
    y_ref[...] = y.astype(y_ref.dtype)


def attention_mixer(qkv, tr=512):
    T = qkv.shape[0]
    outs, lses = [], []
    for gi, (window, dilation) in enumerate(ATT_GROUPS):
        o, lse = dilated_attention_group(qkv, gi, window, dilation)
        outs.append(o)
        lses.append(lse)
    spec = pl.BlockSpec((tr, ATT_OUT), lambda i: (i, 0))
    return pl.pallas_call(
        _attn_merge_kernel,
        out_shape=jax.ShapeDtypeStruct((T, ATT_OUT), BF16),
        grid=(T // tr,),
        in_specs=[spec] * 6,
        out_specs=spec,
        compiler_params=_params(("parallel",)),
        name="attn_merge",
    )(*outs, *lses)


def _head_sum_matrix():
    i = lax.broadcasted_iota(jnp.int32, (LANES, LANES), 0) // RWKV_HEAD
    j = lax.broadcasted_iota(jnp.int32, (LANES, LANES), 1) // RWKV_HEAD
    return jnp.where(i == j, 1.0, 0.0).astype(BF16)


def _head_sums(x, bd):
    cols = []
    for c in range(x.shape[1] // LANES):
        cols.append(_dot_exact_rhs(x[:, c * LANES:(c + 1) * LANES], bd))
    return jnp.concatenate(cols, axis=1)


def _rwkv_prep_kernel(u_ref, mu_ref, w0_ref, w2_ref, a0_ref, a2_ref, g2_ref, kk_ref, ka_ref, rk_ref,
                      r_out, lw_out, k_out, v_out, kkn_out, a_out, g_out, bonus_out, carry_ref):
    i = pl.program_id(0)
    tb = u_ref.shape[0]
    W = RWKV_WIDTH
    CW = 2 * LANES

    @pl.when(i == 0)
    def _():
        carry_ref[...] = jnp.zeros_like(carry_ref)

    def mixed(lo, width):
        sl = slice(lo, lo + width)
        u = u_ref[:, sl]
        row = lax.broadcasted_iota(jnp.int32, u.shape, 0)
        u_prev = jnp.where(row == 0, jnp.broadcast_to(carry_ref[0:1, sl], u.shape), pltpu.roll(u, 1, 0))
        return u + (u_prev - u) * mu_ref[:, sl]

    o = 3 * W
    th = jnp.tanh(mixed(o, RWKV_LORA_W))
    xa = mixed(o + RWKV_LORA_W, RWKV_LORA_A)
    sg = _sigmoid(mixed(o + RWKV_LORA_W + RWKV_LORA_A, RWKV_G_PAD))
    bd = _head_sum_matrix()
    for cc in range(W // CW):
        sl = slice(cc * CW, (cc + 1) * CW)
        r = mixed(cc * CW, CW)
        k = mixed(W + cc * CW, CW)
        v = mixed(2 * W + cc * CW, CW)
        w_log = -_softplus(-(w0_ref[:, sl] + _dot_hi(th, w2_ref[:, sl]))) - 0.5
        lw_out[:, sl] = -jnp.exp(w_log)
        a = _sigmoid(a0_ref[:, sl] + _dot_hi(xa, a2_ref[:, sl]))
        g_out[:, sl] = _dot_hi(sg, g2_ref[:, sl])
        kk = k * kk_ref[:, sl]
        kk = kk / jnp.maximum(jnp.sqrt(_head_sums(kk * kk, bd)), 1e-12)
        k2 = k * (1.0 + (a - 1.0) * ka_ref[:, sl])
        bonus_out[:, sl] = _head_sums(r * k2 * rk_ref[:, sl], bd) * v
        r_out[:, sl] = r
        k_out[:, sl] = k2
        v_out[:, sl] = v
        kkn_out[:, sl] = kk
        a_out[:, sl] = a
    carry_ref[0:1, :] = u_ref[tb - 1:tb, :]


def rwkv_prep(u, mu, w0, w2, a0, a2, g2, k_k, k_a, r_k, tb=128):
    T = u.shape[0]
    W = RWKV_WIDTH
    row = lambda n: pl.BlockSpec((1, n), lambda i: (0, 0))
    full = lambda a: pl.BlockSpec(a.shape, lambda i: (0, 0))
    out_spec = pl.BlockSpec((tb, W), lambda i: (i, 0))
    return pl.pallas_call(
        _rwkv_prep_kernel,
        out_shape=[jax.ShapeDtypeStruct((T, W), F32)] * 8,
        grid=(T // tb,),
        in_specs=[pl.BlockSpec((tb, RWKV_PAD), lambda i: (i, 0)), row(RWKV_PAD), row(W), full(w2),
                  row(W), full(a2), full(g2), row(W), row(W), row(W)],
        out_specs=[out_spec] * 8,
        scratch_shapes=[pltpu.VMEM((8, RWKV_PAD), F32)],
        compiler_params=_params(("arbitrary",)),
        name="rwkv_prep",
    )(u, mu.reshape(1, -1), w0.reshape(1, W), w2, a0.reshape(1, W), a2, g2,
      k_k.reshape(1, W), k_a.reshape(1, W), r_k.reshape(1, W))


def _rwkv_pair_chunk(r, lw, k, v, kk, a, S, consts):
    tril_l, m0, strict, incl, eye = consts
    L = r.shape[0]
    c = _dot_exact_lhs(tril_l, lw)
    c_last = c[L - 1:L, :]
    p_in = jnp.exp(c)
    n_in = jnp.exp(-c)
    to_end = jnp.exp(c_last - c)
    a_t = -kk * jnp.exp(c - lw)
    b_raw = kk * a
    b_t = b_raw * n_in
    k_t = k * n_in
    r_t = r * p_in
    b_h = b_raw * to_end
    k_h = k * to_end

    def stack(x):
        return jnp.concatenate([jnp.where(m0, x, 0.0), jnp.where(m0, 0.0, x)], axis=0)

    a_s, r_s, v_s = stack(a_t), stack(r_t), stack(v)
    lhs = jnp.concatenate([a_s, r_s], axis=0)
    rhs = jnp.concatenate([b_t, b_t, k_t, k_t], axis=0)
    big = _bdot_nt(lhs, rhs)
    n2 = 2 * L
    a_ab = jnp.where(strict, big[0:n2, 0:n2], 0.0)
    a_ak = jnp.where(strict, big[0:n2, n2:2 * n2], 0.0)
    m_rb = jnp.where(incl, big[n2:2 * n2, 0:n2], 0.0)
    m_rk = jnp.where(incl, big[n2:2 * n2, n2:2 * n2], 0.0)

    t_inv = eye + a_ab
    pw = _bdot(a_ab, a_ab)
    steps = (L - 1).bit_length() - 1
    for s in range(steps):
        if s < steps - 1:
            res = _bdot(jnp.concatenate([t_inv, pw], axis=0), pw)
            t_inv = t_inv + res[0:n2]
            pw = res[n2:2 * n2]
        else:
            t_inv = t_inv + _bdot(t_inv, pw)

    av = _bdot(a_ak, v_s)
    sol = _bdot(t_inv, jnp.concatenate([a_s, av], axis=1))
    qy = _bdot(m_rb, sol)
    q_s = r_s + qy[:, 0:LANES]
    y_s = qy[:, LANES:2 * LANES] + _bdot(m_rk, v_s)
    q_h = q_s[0:L] + q_s[L:n2]
    y_in = y_s[0:L] + y_s[L:n2]

    lhs2 = jnp.concatenate([stack(b_h), stack(k_h)], axis=0)
    rhs2 = jnp.concatenate([sol, jnp.concatenate([jnp.zeros_like(v_s), v_s], axis=1)], axis=0)
    gh = _bdot_tn(lhs2, rhs2)
    g_mat = gh[:, 0:LANES] + jnp.where(eye > 0, jnp.broadcast_to(jnp.exp(c_last), (LANES, LANES)), 0.0)
    y = _bdot(q_h, S) + y_in
    s_new = _bdot(g_mat, S) + gh[:, LANES:2 * LANES]
    return y, s_new


def _rwkv_scan_kernel(r_ref, lw_ref, k_ref, v_ref, kk_ref, a_ref, y_ref, s_ref, *, pairs):
    c = pl.program_id(1)
    L = r_ref.shape[0]
    n2 = 2 * L

    @pl.when(c == 0)
    def _():
        s_ref[...] = jnp.zeros_like(s_ref)

    ti = lax.broadcasted_iota(jnp.int32, (L, L), 0)
    tj = lax.broadcasted_iota(jnp.int32, (L, L), 1)
    tril_l = jnp.where(ti >= tj, 1.0, 0.0).astype(BF16)
    m0 = lax.broadcasted_iota(jnp.int32, (L, LANES), 1) < RWKV_HEAD
    si = lax.broadcasted_iota(jnp.int32, (n2, n2), 0)
    sj = lax.broadcasted_iota(jnp.int32, (n2, n2), 1)
    same = (si // L) == (sj // L)
    strict = same & ((sj % L) < (si % L))
    incl = same & ((sj % L) <= (si % L))
    eye = jnp.where(si == sj, 1.0, 0.0).astype(F32)
    consts = (tril_l, m0, strict, incl, eye)

    for p in range(pairs):
        sl = slice(p * LANES, (p + 1) * LANES)
        y, s_new = _rwkv_pair_chunk(r_ref[:, sl], lw_ref[:, sl], k_ref[:, sl], v_ref[:, sl],
                                    kk_ref[:, sl], a_ref[:, sl], s_ref[p], consts)
        y_ref[:, sl] = y
        s_ref[p] = s_new


def rwkv_scan(r, lw, k, v, kk, a, pairs=4):
    T, W = r.shape
    L = RWKV_CHUNK
    wb = pairs * LANES
    spec = pl.BlockSpec((L, wb), lambda p, c: (c, p))
    return pl.pallas_call(
        functools.partial(_rwkv_scan_kernel, pairs=pairs),
        out_shape=jax.ShapeDtypeStruct((T, W), F32),
        grid=(W // wb, T // L),
        in_specs=[spec] * 6,
        out_specs=spec,
        scratch_shapes=[pltpu.VMEM((pairs, LANES, LANES), F32)],
        compiler_params=_params(("parallel", "arbitrary")),
        name="rwkv_scan",
    )(r, lw, k, v, kk, a)


def _rwkv_post_kernel(y_ref, bonus_ref, g_ref, lnw_ref, lnb_ref, o_ref):
    bd = _head_sum_matrix()
    y = y_ref[...]
    inv_n = 1.0 / RWKV_HEAD
    mean = _head_sums(y, bd) * inv_n
    yc = y - mean
    var = _head_sums(yc * yc, bd) * inv_n
    out = yc * lax.rsqrt(var + RWKV_LN_EPS) * lnw_ref[...] + lnb_ref[...] + bonus_ref[...]
    o_ref[...] = (out * g_ref[...]).astype(o_ref.dtype)


def rwkv_post(y, bonus, g, ln_w, ln_b, tb=256):
    T, W = y.shape
    spec = pl.BlockSpec((tb, W), lambda i: (i, 0))
    row = pl.BlockSpec((1, W), lambda i: (0, 0))
    return pl.pallas_call(
        _rwkv_post_kernel,
        out_shape=jax.ShapeDtypeStruct((T, W), BF16),
        grid=(T // tb,),
        in_specs=[spec, spec, spec, row, row],
        out_specs=spec,
        compiler_params=_params(("parallel",)),
        name="rwkv_post",
    )(y, bonus, g, ln_w.reshape(1, W), ln_b.reshape(1, W))


def rwkv_mixer(u, mu, w0, w2, a0, a2, g2, k_k, k_a, r_k, ln_w, ln_b):
    r, lw, k, v, kk, a, g, bonus = rwkv_prep(u, mu, w0, w2, a0, a2, g2, k_k, k_a, r_k)
    y = rwkv_scan(r, lw, k, v, kk, a)
    return rwkv_post(y, bonus, g, ln_w, ln_b)


def _ssd_kernel(p_ref, cw_ref, cb_ref, dtb_ref, alog_ref, dsk_ref, ng_ref, o_ref,
                ext_ref, xbc_ref, st_ref, y_scr):
    c = pl.program_id(0)
    L = SSM_CHUNK
    X0 = SSM_INNER
    HP = 2 * 64

    @pl.when(c == 0)
    def _():
        ext_ref[0:8, :] = jnp.zeros((8, SSM_CONV_DIM), F32)
        st_ref[...] = jnp.zeros_like(st_ref)

    ext_ref[8:8 + L, :] = p_ref[:, X0:X0 + SSM_CONV_DIM]
    CW = 4 * LANES
    for cc in range(SSM_CONV_DIM // CW):
        sl = slice(cc * CW, (cc + 1) * CW)
        conv = jnp.broadcast_to(cb_ref[:, sl], (L, CW))
        for kk in range(SSM_CONV):
            conv = conv + cw_ref[kk:kk + 1, sl] * ext_ref[pl.ds(8 - (SSM_CONV - 1) + kk, L), sl]
        xbc_ref[:, sl] = _silu(conv)
    ext_ref[0:8, :] = p_ref[L - 8:L, X0:X0 + SSM_CONV_DIM]

    dt = _softplus(p_ref[:, X0 + SSM_CONV_DIM:X0 + SSM_CONV_DIM + LANES] + dtb_ref[...])
    d_a = dt * (-jnp.exp(alog_ref[...]))
    ti = lax.broadcasted_iota(jnp.int32, (L, L), 0)
    tj = lax.broadcasted_iota(jnp.int32, (L, L), 1)
    causal = ti >= tj
    a_cum = _dot_exact_lhs(jnp.where(causal, 1.0, 0.0).astype(BF16), d_a)
    a_cum_t = a_cum.T
    dt_t = dt.T
    lane = lax.broadcasted_iota(jnp.int32, (L, HP), 1)
    first = lane < 64

    for g in range(SSM_GROUPS):
        b_m = xbc_ref[:,SSM_INNER + g * SSM_STATE:SSM_INNER + (g + 1) * SSM_STATE]
        c_m = xbc_ref[:,SSM_INNER + SSM_GROUPS * SSM_STATE + g * SSM_STATE:
                  SSM_INNER + SSM_GROUPS * SSM_STATE + (g + 1) * SSM_STATE]
        cb = _bdot_nt(c_m, b_m)
        b_t = b_m.T
        for j in range(3):
            pair = g * 3 + j
            xs = xbc_ref[:,pair * HP:(pair + 1) * HP]
            yd, st, ea, cd = [], [], [], []
            for hh in (2 * pair, 2 * pair + 1):
                col = a_cum[:, hh:hh + 1]
                row = a_cum_t[hh:hh + 1, :]
                dtr = dt_t[hh:hh + 1, :]
                last = a_cum[L - 1:L, hh:hh + 1]
                lmat = jnp.exp(jnp.where(causal, col - row, -jnp.inf))
                yd.append(_bdot(cb * lmat * dtr, xs))
                st.append(_bdot(b_t * (jnp.exp(last - row) * dtr), xs))
                ea.append(jnp.exp(col))
                cd.append(jnp.exp(last))
            s_in = st_ref[pair]
            y = jnp.where(first, yd[0], yd[1])
            y = y + _bdot(c_m, s_in) * jnp.where(first, ea[0], ea[1])
            st_ref[pair] = s_in * jnp.where(first, cd[0], cd[1]) + jnp.where(first, st[0], st[1])
            y_scr[:, pair * HP:(pair + 1) * HP] = y + xs * dsk_ref[:, pair * HP:(pair + 1) * HP]

    gw = SSM_INNER // SSM_GROUPS
    for g in range(SSM_GROUPS):
        sl = slice(g * gw, (g + 1) * gw)
        yg = y_scr[:, sl] * _silu(p_ref[:, sl])
        ms = jnp.mean(yg * yg, axis=-1, keepdims=True)
        o_ref[:, sl] = (yg * lax.rsqrt(ms + SSM_NORM_EPS) * ng_ref[:, sl]).astype(o_ref.dtype)


def mamba2_mixer(u, conv_w, conv_b, dt_bias, a_log, d_skip, norm_g):
    T = u.shape[0]
    L = SSM_CHUNK
    padh = lambda x: jnp.pad(x, (0, LANES - SSM_HEADS)).reshape(1, LANES)
    full = lambda a: pl.BlockSpec(a.shape, lambda i: (0,) * a.ndim)
    args = (u, conv_w, conv_b.reshape(1, -1), padh(dt_bias), padh(a_log),
            jnp.repeat(d_skip, 64).reshape(1, SSM_INNER), norm_g.reshape(1, SSM_INNER))
    return pl.pallas_call(
        _ssd_kernel,
        out_shape=jax.ShapeDtypeStruct((T, SSM_INNER), BF16),
        grid=(T // L,),
        in_specs=[pl.BlockSpec((L, SSM_PAD), lambda i: (i, 0))] + [full(a) for a in args[1:]],
        out_specs=pl.BlockSpec((L, SSM_INNER), lambda i: (i, 0)),
        scratch_shapes=[pltpu.VMEM((L + 8, SSM_CONV_DIM), F32),
                        pltpu.VMEM((L, SSM_CONV_DIM), F32),
                        pltpu.VMEM((SSM_HEADS // 2, SSM_STATE, LANES), F32),
                        pltpu.VMEM((L, SSM_INNER), F32)],
        compiler_params=_params(("arbitrary",)),
        name="ssd",
    )(*args)


def _pad_cols(w, n):
    return jnp.pad(w, ((0, 0), (0, n - w.shape[1])))


def kernel(x, norm_mix_g, w_in, rwkv_mu, rwkv_w0, rwkv_w2, rwkv_a0, rwkv_a2, rwkv_g2, rwkv_k_k, rwkv_k_a, rwkv_r_k, rwkv_ln_w, rwkv_ln_b, ssm_conv_w, ssm_conv_b, ssm_dt_bias, ssm_a_log, ssm_d, ssm_norm_g, p_attn, p_rwkv, p_ssm, w_out, norm_ffn_g, w_ffn_gate, w_ffn_up, w_ffn_down, norm_final_g):
    B, T, D = x.shape
    assert B == 1 and D == D_MODEL
    xt = x.reshape(T, D)
    o_r = ATT_COLS
    o_s = ATT_COLS + RWKV_COLS
    o_g = ATT_COLS + RWKV_COLS + SSM_COLS
    TM, TN = 1024, 512
    for l in range(DEPTH):
        wl = w_in[l]
        w_att = wl[:, :o_r].astype(BF16)
        w_rwkv = _pad_cols(wl[:, o_r:o_s], RWKV_PAD).astype(BF16)
        w_ssm = _pad_cols(wl[:, o_s:o_g], SSM_PAD).astype(BF16)
        w_gate = wl[:, o_g:].astype(BF16)

        h = rms_norm(xt, norm_mix_g[l], BF16)
        qkv = matmul(h, w_att, BF16, TM, TN, "proj_att")
        u_rwkv = matmul(h, w_rwkv, F32, TM, TN, "proj_rwkv")
        u_ssm = matmul(h, w_ssm, F32, TM, TN, "proj_ssm")
        gates = matmul(h, w_gate, F32, TM, TN, "proj_gate")

        y_att = attention_mixer(qkv)
        g2 = jnp.pad(rwkv_g2[l], ((0, RWKV_G_PAD - RWKV_LORA_G), (0, 0)))
        mu = jnp.pad(rwkv_mu[l], (0, RWKV_PAD - RWKV_COLS))
        y_rwkv = rwkv_mixer(u_rwkv, mu, rwkv_w0[l], rwkv_w2[l], rwkv_a0[l], rwkv_a2[l], g2,
                            rwkv_k_k[l], rwkv_k_a[l], rwkv_r_k[l].reshape(-1), rwkv_ln_w[l], rwkv_ln_b[l])
        y_ssm = mamba2_mixer(u_ssm, ssm_conv_w[l], ssm_conv_b[l], ssm_dt_bias[l], ssm_a_log[l],
                             ssm_d[l], ssm_norm_g[l])

        merged = merge_proj(y_att, y_rwkv, y_ssm, p_attn[l].astype(BF16), p_rwkv[l].astype(BF16),
                            p_ssm[l].astype(BF16), gates, TM // 2, TN)
        xt = matmul_residual(merged, w_out[l].astype(BF16), xt, TM, TN, "w_out")

        h2 = rms_norm(xt, norm_ffn_g[l], BF16)
        wg = _pad_cols(w_ffn_gate[l], FFN_PAD).astype(BF16)
        wu = _pad_cols(w_ffn_up[l], FFN_PAD).astype(BF16)
        wd = jnp.pad(w_ffn_down[l], ((0, FFN_PAD - FFN_HIDDEN), (0, 0))).astype(BF16)
        act = ffn_up(h2, wg, wu, TM, TN)
        xt = matmul_residual_ksplit(act, wd, xt, TM, TN, FFN_PAD // 4, "ffn_down")
    out = rms_norm(xt, norm_final_g, F32)
    return out.reshape(B, T, D)
```

```python
import functools

import jax
import jax.numpy as jnp
from jax import lax
from jax.experimental import pallas as pl
from jax.experimental.pallas import tpu as pltpu

F32 = jnp.float32
BF16 = jnp.bfloat16

D_MODEL = 4096
DEPTH = 2
NORM_EPS = 1e-6
ATT_GROUPS = ((128, 1), (512, 4), (2048, 16))
ATT_HEADS = 8
ATT_DIM = 128
ATT_WIDTH = 3072
ATT_OUT = 1024
ATT_COLS = 9216
RWKV_WIDTH = 3072
RWKV_HEAD = 64
RWKV_LORA_W = 128
RWKV_LORA_A = 128
RWKV_LORA_G = 480
RWKV_COLS = 9952
RWKV_LN_EPS = 64e-5
SSM_INNER = 3072
SSM_HEADS = 48
SSM_GROUPS = 8
SSM_STATE = 128
SSM_CONV = 4
SSM_CHUNK = 128
SSM_CONV_DIM = 5120
SSM_COLS = 8240
SSM_NORM_EPS = 1e-5
FFN_HIDDEN = 11008

RWKV_PAD = 10240
RWKV_G_PAD = 512
SSM_PAD = 8704

LANES = 128
RWKV_CHUNK = 64
VMEM_LIMIT = 56 * 1024 * 1024


def _params(sem):
    return pltpu.CompilerParams(dimension_semantics=sem, vmem_limit_bytes=VMEM_LIMIT)


def _bdot(a, b):
    return jnp.dot(a.astype(BF16), b.astype(BF16), preferred_element_type=F32)


def _bdot_nt(a, b):
    return lax.dot_general(a.astype(BF16), b.astype(BF16), (((1,), (1,)), ((), ())),
                           preferred_element_type=F32)


def _bdot_tn(a, b):
    return lax.dot_general(a.astype(BF16), b.astype(BF16), (((0,), (0,)), ((), ())),
                           preferred_element_type=F32)


def _split2(x):
    hi = x.astype(BF16)
    lo = (x - hi.astype(F32)).astype(BF16)
    return hi, lo


def _split3(x):
    hi = x.astype(BF16)
    r1 = x - hi.astype(F32)
    mid = r1.astype(BF16)
    lo = (r1 - mid.astype(F32)).astype(BF16)
    return hi, mid, lo


def _dot_exact_lhs(m_bf16, x):
    hi, mid, lo = _split3(x)
    d = functools.partial(jnp.dot, preferred_element_type=F32)
    return d(m_bf16, hi) + d(m_bf16, mid) + d(m_bf16, lo)


def _dot_exact_rhs(x, m_bf16):
    hi, mid, lo = _split3(x)
    d = functools.partial(jnp.dot, preferred_element_type=F32)
    return d(hi, m_bf16) + d(mid, m_bf16) + d(lo, m_bf16)


def _dot_hi(a, b):
    ah, al = _split2(a)
    bh, bl = _split2(b)
    d = functools.partial(jnp.dot, preferred_element_type=F32)
    return d(ah, bh) + d(ah, bl) + d(al, bh)


def _sigmoid(x):
    return 1.0 / (1.0 + jnp.exp(-x))


def _silu(x):
    return x * _sigmoid(x)


def _softplus(x):
    return jnp.maximum(x, 0.0) + jnp.log(1.0 + jnp.exp(-jnp.abs(x)))


def _rms_kernel(x_ref, g_ref, o_ref):
    x = x_ref[...]
    ms = jnp.mean(x * x, axis=-1, keepdims=True)
    o_ref[...] = (x * lax.rsqrt(ms + NORM_EPS) * g_ref[...]).astype(o_ref.dtype)


def rms_norm(x, g, out_dtype, tr=256):
    T, D = x.shape
    return pl.pallas_call(
        _rms_kernel,
        out_shape=jax.ShapeDtypeStruct((T, D), out_dtype),
        grid=(T // tr,),
        in_specs=[pl.BlockSpec((tr, D), lambda i: (i, 0)),
                  pl.BlockSpec((1, D), lambda i: (0, 0))],
        out_specs=pl.BlockSpec((tr, D), lambda i: (i, 0)),
        compiler_params=_params(("parallel",)),
        name="rms_norm",
    )(x, g.reshape(1, D))


def _mm_kernel(a_ref, w_ref, o_ref):
    o_ref[...] = jnp.dot(a_ref[...], w_ref[...], preferred_element_type=F32).astype(o_ref.dtype)


def matmul(a, w, out_dtype, tm, tn, name):
    M, K = a.shape
    _, N = w.shape
    return pl.pallas_call(
        _mm_kernel,
        out_shape=jax.ShapeDtypeStruct((M, N), out_dtype),
        grid=(M // tm, N // tn),
        in_specs=[pl.BlockSpec((tm, K), lambda i, j: (i, 0)),
                  pl.BlockSpec((K, tn), lambda i, j: (0, j))],
        out_specs=pl.BlockSpec((tm, tn), lambda i, j: (i, j)),
        compiler_params=_params(("parallel", "arbitrary")),
        name=name,
    )(a, w)


def _mm_ws_kernel(a_ref, w_ref, o_ref, wb_ref):
    @pl.when(pl.program_id(1) == 0)
    def _():
        wb_ref[...] = w_ref[...].astype(BF16)

    o_ref[...] = jnp.dot(a_ref[...], wb_ref[...], preferred_element_type=F32).astype(o_ref.dtype)


def matmul_ws(a, w_stack, layer, col0, n_out, out_dtype, tm, tn, name):
    M, K = a.shape
    cb = col0 // tn
    return pl.pallas_call(
        _mm_ws_kernel,
        out_shape=jax.ShapeDtypeStruct((M, n_out), out_dtype),
        grid=(n_out // tn, M // tm),
        in_specs=[pl.BlockSpec((tm, K), lambda j, i: (i, 0)),
                  pl.BlockSpec((None, K, tn), lambda j, i: (layer, 0, j + cb))],
        out_specs=pl.BlockSpec((tm, tn), lambda j, i: (i, j)),
        scratch_shapes=[pltpu.VMEM((K, tn), BF16)],
        compiler_params=_params(("arbitrary", "arbitrary")),
        name=name,
    )(a, w_stack)


def _ffn_up_ws_kernel(h_ref, wg_ref, wu_ref, o_ref, wgb_ref, wub_ref):
    @pl.when(pl.program_id(1) == 0)
    def _():
        wgb_ref[...] = wg_ref[...].astype(BF16)
        wub_ref[...] = wu_ref[...].astype(BF16)

    h = h_ref[...]
    gate = jnp.dot(h, wgb_ref[...], preferred_element_type=F32)
    up = jnp.dot(h, wub_ref[...], preferred_element_type=F32)
    o_ref[...] = (_silu(gate) * up).astype(o_ref.dtype)


def ffn_up_ws(h, wg_stack, wu_stack, layer, tm, tn):
    M, K = h.shape
    N = wg_stack.shape[2]
    wspec = pl.BlockSpec((None, K, tn), lambda j, i: (layer, 0, j))
    return pl.pallas_call(
        _ffn_up_ws_kernel,
        out_shape=jax.ShapeDtypeStruct((M, N), BF16),
        grid=(N // tn, M // tm),
        in_specs=[pl.BlockSpec((tm, K), lambda j, i: (i, 0)), wspec, wspec],
        out_specs=pl.BlockSpec((tm, tn), lambda j, i: (i, j)),
        scratch_shapes=[pltpu.VMEM((K, tn), BF16)] * 2,
        compiler_params=_params(("arbitrary", "arbitrary")),
        name="ffn_up",
    )(h, wg_stack, wu_stack)


def _mm_res_kernel(a_ref, w_ref, x_ref, o_ref):
    o_ref[...] = x_ref[...] + jnp.dot(a_ref[...], w_ref[...], preferred_element_type=F32)


def matmul_residual(a, w, x, tm, tn, name):
    M, K = a.shape
    _, N = w.shape
    return pl.pallas_call(
        _mm_res_kernel,
        out_shape=jax.ShapeDtypeStruct((M, N), F32),
        grid=(M // tm, N // tn),
        in_specs=[pl.BlockSpec((tm, K), lambda i, j: (i, 0)),
                  pl.BlockSpec((K, tn), lambda i, j: (0, j)),
                  pl.BlockSpec((tm, tn), lambda i, j: (i, j))],
        out_specs=pl.BlockSpec((tm, tn), lambda i, j: (i, j)),
        compiler_params=_params(("parallel", "arbitrary")),
        name=name,
    )(a, w, x)


def _merge_proj_kernel(ya_ref, yr_ref, ys_ref, pa_ref, pr_ref, ps_ref,
                       ga_ref, gr_ref, gs_ref, o_ref):
    acc = _sigmoid(ga_ref[...]) * jnp.dot(ya_ref[...], pa_ref[...], preferred_element_type=F32)
    acc += _sigmoid(gr_ref[...]) * jnp.dot(yr_ref[...], pr_ref[...], preferred_element_type=F32)
    acc += _sigmoid(gs_ref[...]) * jnp.dot(ys_ref[...], ps_ref[...], preferred_element_type=F32)
    o_ref[...] = acc.astype(o_ref.dtype)


def merge_proj(y_att, y_rwkv, y_ssm, p_att, p_rwkv, p_ssm, gates, tm, tn):
    M = y_att.shape[0]
    N = p_att.shape[1]
    nb = N // tn
    return pl.pallas_call(
        _merge_proj_kernel,
        out_shape=jax.ShapeDtypeStruct((M, N), BF16),
        grid=(M // tm, nb),
        in_specs=[pl.BlockSpec((tm, y_att.shape[1]), lambda i, j: (i, 0)),
                  pl.BlockSpec((tm, y_rwkv.shape[1]), lambda i, j: (i, 0)),
                  pl.BlockSpec((tm, y_ssm.shape[1]), lambda i, j: (i, 0)),
                  pl.BlockSpec((p_att.shape[0], tn), lambda i, j: (0, j)),
                  pl.BlockSpec((p_rwkv.shape[0], tn), lambda i, j: (0, j)),
                  pl.BlockSpec((p_ssm.shape[0], tn), lambda i, j: (0, j)),
                  pl.BlockSpec((tm, tn), lambda i, j: (i, j)),
                  pl.BlockSpec((tm, tn), lambda i, j: (i, j + nb)),
                  pl.BlockSpec((tm, tn), lambda i, j: (i, j + 2 * nb))],
        out_specs=pl.BlockSpec((tm, tn), lambda i, j: (i, j)),
        compiler_params=_params(("parallel", "arbitrary")),
        name="merge_proj",
    )(y_att, y_rwkv, y_ssm, p_att, p_rwkv, p_ssm, gates, gates, gates)


ATT_WIN = 2048
ATT_UNROLL = 4


def _rows(start, size, stride):
    return pl.ds(start, size) if stride == 1 else pl.ds(start, size, stride=stride)


def _attn_kernel(*refs):
    n_g = len(ATT_GROUPS)
    ins = refs[:5 * n_g]
    y_ref = refs[5 * n_g]
    scr = refs[5 * n_g + 1:]
    kext, vext = scr[0:n_g], scr[n_g:2 * n_g]
    o_scr, l_scr = scr[2 * n_g], scr[2 * n_g + 1]
    w = pl.program_id(1)
    blk = ATT_DIM
    qi = lax.broadcasted_iota(jnp.int32, (blk, 2 * blk), 0)
    kj = lax.broadcasted_iota(jnp.int32, (blk, 2 * blk), 1)
    dist = qi + blk - kj
    scale = ATT_DIM ** -0.5

    for g, (window, d) in enumerate(ATT_GROUPS):
        q_ref, k_ref, v_ref, kp_ref, vp_ref = ins[5 * g:5 * g + 5]
        halo = d * blk
        kext[g][0:halo, :] = kp_ref[...]
        kext[g][halo:halo + ATT_WIN, :] = k_ref[...]
        vext[g][0:halo, :] = vp_ref[...]
        vext[g][halo:halo + ATT_WIN, :] = v_ref[...]
        band = (dist >= 0) & (dist <= window // d)
        shift = d.bit_length() - 1

        def body(it, carry, g=g, d=d, q_ref=q_ref, band=band, shift=shift, halo=halo):
            for u in range(ATT_UNROLL):
                idx = it * ATT_UNROLL + u
                res = jnp.bitwise_and(idx, d - 1)
                m = jnp.right_shift(idx, shift)
                row0 = res + halo * m
                q = q_ref[_rows(row0, blk, d), :].astype(BF16)
                k = kext[g][_rows(row0, 2 * blk, d), :].astype(BF16)
                v = vext[g][_rows(row0, 2 * blk, d), :].astype(BF16)
                s = lax.dot_general(q, k, (((1,), (1,)), ((), ())), preferred_element_type=F32) * scale
                has_prev = jnp.logical_or(w > 0, m > 0)
                s = jnp.where(band & jnp.logical_or(has_prev, kj >= blk), s, -jnp.inf)
                mx = jnp.max(s, axis=-1, keepdims=True)
                p = jnp.exp(s - mx)
                den = jnp.sum(p, axis=-1, keepdims=True)
                o = jnp.dot((p / den).astype(BF16), v, preferred_element_type=F32)
                o_scr[g, _rows(row0, blk, d), :] = o
                l_scr[g, _rows(row0, blk, d), :] = jnp.broadcast_to(mx + jnp.log(den), (blk, blk))
            return carry

        lax.fori_loop(0, ATT_WIN // (blk * ATT_UNROLL), body, 0)

    rc = 256
    for c in range(ATT_WIN // rc):
        rows = slice(c * rc, (c + 1) * rc)
        la, lb, lc = l_scr[0, rows, :], l_scr[1, rows, :], l_scr[2, rows, :]
        mx = jnp.maximum(jnp.maximum(la, lb), lc)
        ea, eb, ec = jnp.exp(la - mx), jnp.exp(lb - mx), jnp.exp(lc - mx)
        y = (ea * o_scr[0, rows, :] + eb * o_scr[1, rows, :] + ec * o_scr[2, rows, :]) / (ea + eb + ec)
        y_ref[rows, :] = y.astype(y_ref.dtype)


def attention_mixer(qkv):
    T = qkv.shape[0]
    blk = ATT_DIM
    n_g = len(ATT_GROUPS)
    sect = n_g * ATT_HEADS
    in_specs, scratch_k, scratch_v = [], [], []
    for g, (window, d) in enumerate(ATT_GROUPS):
        per_win = ATT_WIN // (d * blk)
        for s in range(3):
            in_specs.append(pl.BlockSpec((ATT_WIN, blk),
                                         lambda h, w, s=s, g=g: (w, s * sect + g * ATT_HEADS + h)))
        for s in (1, 2):
            in_specs.append(pl.BlockSpec(
                (d * blk, blk),
                lambda h, w, s=s, g=g, per_win=per_win: (jnp.maximum(w * per_win - 1, 0),
                                                         s * sect + g * ATT_HEADS + h)))
        scratch_k.append(pltpu.VMEM((ATT_WIN + d * blk, blk), F32))
        scratch_v.append(pltpu.VMEM((ATT_WIN + d * blk, blk), F32))
    return pl.pallas_call(
        _attn_kernel,
        out_shape=jax.ShapeDtypeStruct((T, ATT_OUT), BF16),
        grid=(ATT_HEADS, T // ATT_WIN),
        in_specs=in_specs,
        out_specs=pl.BlockSpec((ATT_WIN, blk), lambda h, w: (w, h)),
        scratch_shapes=scratch_k + scratch_v + [pltpu.VMEM((n_g, ATT_WIN, blk), F32)] * 2,
        compiler_params=_params(("parallel", "arbitrary")),
        name="attention",
    )(*([qkv] * (5 * n_g)))


def _head_sum_matrix():
    i = lax.broadcasted_iota(jnp.int32, (LANES, LANES), 0) // RWKV_HEAD
    j = lax.broadcasted_iota(jnp.int32, (LANES, LANES), 1) // RWKV_HEAD
    return jnp.where(i == j, 1.0, 0.0).astype(BF16)


def _head_sums(x, bd):
    cols = []
    for c in range(x.shape[1] // LANES):
        cols.append(_dot_exact_rhs(x[:, c * LANES:(c + 1) * LANES], bd))
    return jnp.concatenate(cols, axis=1)


def _rwkv_prep_kernel(u_ref, mu_ref, w0_ref, w2_ref, a0_ref, a2_ref, g2_ref, kk_ref, ka_ref, rk_ref,
                      r_out, lw_out, k_out, v_out, kkn_out, a_out, g_out, bonus_out, carry_ref):
    i = pl.program_id(0)
    tb = u_ref.shape[0]
    W = RWKV_WIDTH
    CW = 2 * LANES

    @pl.when(i == 0)
    def _():
        carry_ref[...] = jnp.zeros_like(carry_ref)

    def mixed(lo, width):
        sl = slice(lo, lo + width)
        u = u_ref[:, sl]
        row = lax.broadcasted_iota(jnp.int32, u.shape, 0)
        u_prev = jnp.where(row == 0, jnp.broadcast_to(carry_ref[0:1, sl], u.shape), pltpu.roll(u, 1, 0))
        return u + (u_prev - u) * mu_ref[:, sl]

    o = 3 * W
    th = jnp.tanh(mixed(o, RWKV_LORA_W))
    xa = mixed(o + RWKV_LORA_W, RWKV_LORA_A)
    sg = _sigmoid(mixed(o + RWKV_LORA_W + RWKV_LORA_A, RWKV_G_PAD))
    bd = _head_sum_matrix()
    for cc in range(W // CW):
        sl = slice(cc * CW, (cc + 1) * CW)
        r = mixed(cc * CW, CW)
        k = mixed(W + cc * CW, CW)
        v = mixed(2 * W + cc * CW, CW)
        w_log = -_softplus(-(w0_ref[:, sl] + _dot_hi(th, w2_ref[:, sl]))) - 0.5
        lw_out[:, sl] = -jnp.exp(w_log)
        a = _sigmoid(a0_ref[:, sl] + _dot_hi(xa, a2_ref[:, sl]))
        g_out[:, sl] = _dot_hi(sg, g2_ref[:, sl])
        kk = k * kk_ref[:, sl]
        kk = kk / jnp.maximum(jnp.sqrt(_head_sums(kk * kk, bd)), 1e-12)
        k2 = k * (1.0 + (a - 1.0) * ka_ref[:, sl])
        bonus_out[:, sl] = _head_sums(r * k2 * rk_ref[:, sl], bd) * v
        r_out[:, sl] = r
        k_out[:, sl] = k2
        v_out[:, sl] = v
        kkn_out[:, sl] = kk
        a_out[:, sl] = a
    carry_ref[0:1, :] = u_ref[tb - 1:tb, :]


def rwkv_prep(u, mu, w0, w2, a0, a2, g2, k_k, k_a, r_k, tb=128):
    T = u.shape[0]
    W = RWKV_WIDTH
    row = lambda n: pl.BlockSpec((1, n), lambda i: (0, 0))
    full = lambda a: pl.BlockSpec(a.shape, lambda i: (0, 0))
    out_spec = pl.BlockSpec((tb, W), lambda i: (i, 0))
    return pl.pallas_call(
        _rwkv_prep_kernel,
        out_shape=[jax.ShapeDtypeStruct((T, W), F32)] * 8,
        grid=(T // tb,),
        in_specs=[pl.BlockSpec((tb, RWKV_PAD), lambda i: (i, 0)), row(RWKV_PAD), row(W), full(w2),
                  row(W), full(a2), full(g2), row(W), row(W), row(W)],
        out_specs=[out_spec] * 8,
        scratch_shapes=[pltpu.VMEM((8, RWKV_PAD), F32)],
        compiler_params=_params(("arbitrary",)),
        name="rwkv_prep",
    )(u, mu.reshape(1, -1), w0.reshape(1, W), w2, a0.reshape(1, W), a2, g2,
      k_k.reshape(1, W), k_a.reshape(1, W), r_k.reshape(1, W))


def _rwkv_chunk(ins, states, consts):
    tril_l, m0, strict, incl, eye = consts
    P = range(len(ins))
    L = ins[0][0].shape[0]
    n2 = 2 * L

    def stack(x):
        return jnp.concatenate([jnp.where(m0, x, 0.0), jnp.where(m0, 0.0, x)], axis=0)

    c = [_dot_exact_lhs(tril_l, ins[p][1]) for p in P]
    c_last = [c[p][L - 1:L, :] for p in P]
    lhs, rhs, r_s, a_s, v_s, lhs2 = [], [], [], [], [], []
    for p in P:
        r, lw, k, v, kk, a = ins[p]
        n_in = jnp.exp(-c[p])
        to_end = jnp.exp(c_last[p] - c[p])
        a_t = -kk * jnp.exp(c[p] - lw)
        b_raw = kk * a
        b_t = b_raw * n_in
        k_t = k * n_in
        a_s.append(stack(a_t))
        r_s.append(stack(r * jnp.exp(c[p])))
        v_s.append(stack(v))
        lhs.append(jnp.concatenate([a_s[p], r_s[p]], axis=0))
        rhs.append(jnp.concatenate([b_t, b_t, k_t, k_t], axis=0))
        lhs2.append(jnp.concatenate([stack(b_raw * to_end), stack(k * to_end)], axis=0))
    big = [_bdot_nt(lhs[p], rhs[p]) for p in P]
    a_ab = [jnp.where(strict, big[p][0:n2, 0:n2], 0.0) for p in P]
    a_ak = [jnp.where(strict, big[p][0:n2, n2:2 * n2], 0.0) for p in P]
    m_rb = [jnp.where(incl, big[p][n2:2 * n2, 0:n2], 0.0) for p in P]
    m_rk = [jnp.where(incl, big[p][n2:2 * n2, n2:2 * n2], 0.0) for p in P]

    t_inv = [eye + a_ab[p] for p in P]
    pw = [_bdot(a_ab[p], a_ab[p]) for p in P]
    av = [_bdot(a_ak[p], v_s[p]) for p in P]
    steps = (L - 1).bit_length() - 1
    for s in range(steps):
        if s < steps - 1:
            res = [_bdot(jnp.concatenate([t_inv[p], pw[p]], axis=0), pw[p]) for p in P]
            t_inv = [t_inv[p] + res[p][0:n2] for p in P]
            pw = [res[p][n2:2 * n2] for p in P]
        else:
            res = [_bdot(t_inv[p], pw[p]) for p in P]
            t_inv = [t_inv[p] + res[p] for p in P]

    sol = [_bdot(t_inv[p], jnp.concatenate([a_s[p], av[p]], axis=1)) for p in P]
    qy = [_bdot(m_rb[p], sol[p]) for p in P]
    yv = [_bdot(m_rk[p], v_s[p]) for p in P]
    gh = [_bdot_tn(lhs2[p], jnp.concatenate(
        [sol[p], jnp.concatenate([jnp.zeros_like(v_s[p]), v_s[p]], axis=1)], axis=0)) for p in P]
    outs = []
    for p in P:
        q_s = r_s[p] + qy[p][:, 0:LANES]
        y_s = qy[p][:, LANES:2 * LANES] + yv[p]
        q_h = q_s[0:L] + q_s[L:n2]
        y_in = y_s[0:L] + y_s[L:n2]
        g_mat = gh[p][:, 0:LANES] + jnp.where(
            eye > 0, jnp.broadcast_to(jnp.exp(c_last[p]), (LANES, LANES)), 0.0)
        y = _bdot(q_h, states[p]) + y_in
        s_new = _bdot(g_mat, states[p]) + gh[p][:, LANES:2 * LANES]
        outs.append((y, s_new))
    return outs


def _rwkv_scan_kernel(r_ref, lw_ref, k_ref, v_ref, kk_ref, a_ref, y_ref, s_ref, *, pairs):
    c = pl.program_id(1)
    L = r_ref.shape[0]
    n2 = 2 * L

    @pl.when(c == 0)
    def _():
        s_ref[...] = jnp.zeros_like(s_ref)

    ti = lax.broadcasted_iota(jnp.int32, (L, L), 0)
    tj = lax.broadcasted_iota(jnp.int32, (L, L), 1)
    tril_l = jnp.where(ti >= tj, 1.0, 0.0).astype(BF16)
    m0 = lax.broadcasted_iota(jnp.int32, (L, LANES), 1) < RWKV_HEAD
    si = lax.broadcasted_iota(jnp.int32, (n2, n2), 0)
    sj = lax.broadcasted_iota(jnp.int32, (n2, n2), 1)
    same = (si // L) == (sj // L)
    strict = same & ((sj % L) < (si % L))
    incl = same & ((sj % L) <= (si % L))
    eye = jnp.where(si == sj, 1.0, 0.0).astype(F32)
    consts = (tril_l, m0, strict, incl, eye)

    sls = [slice(p * LANES, (p + 1) * LANES) for p in range(pairs)]
    ins = [(r_ref[:, sl], lw_ref[:, sl], k_ref[:, sl], v_ref[:, sl], kk_ref[:, sl], a_ref[:, sl])
           for sl in sls]
    outs = _rwkv_chunk(ins, [s_ref[p] for p in range(pairs)], consts)
    for p in range(pairs):
        y_ref[:, sls[p]] = outs[p][0]
        s_ref[p] = outs[p][1]


def rwkv_scan(r, lw, k, v, kk, a, pairs=8):
    T, W = r.shape
    L = RWKV_CHUNK
    wb = pairs * LANES
    spec = pl.BlockSpec((L, wb), lambda p, c: (c, p))
    return pl.pallas_call(
        functools.partial(_rwkv_scan_kernel, pairs=pairs),
        out_shape=jax.ShapeDtypeStruct((T, W), F32),
        grid=(W // wb, T // L),
        in_specs=[spec] * 6,
        out_specs=spec,
        scratch_shapes=[pltpu.VMEM((pairs, LANES, LANES), F32)],
        compiler_params=_params(("parallel", "arbitrary")),
        name="rwkv_scan",
    )(r, lw, k, v, kk, a)


def _rwkv_post_kernel(y_ref, bonus_ref, g_ref, lnw_ref, lnb_ref, o_ref):
    bd = _head_sum_matrix()
    y = y_ref[...]
    inv_n = 1.0 / RWKV_HEAD
    mean = _head_sums(y, bd) * inv_n
    yc = y - mean
    var = _head_sums(yc * yc, bd) * inv_n
    out = yc * lax.rsqrt(var + RWKV_LN_EPS) * lnw_ref[...] + lnb_ref[...] + bonus_ref[...]
    o_ref[...] = (out * g_ref[...]).astype(o_ref.dtype)


def rwkv_post(y, bonus, g, ln_w, ln_b, tb=256):
    T, W = y.shape
    spec = pl.BlockSpec((tb, W), lambda i: (i, 0))
    row = pl.BlockSpec((1, W), lambda i: (0, 0))
    return pl.pallas_call(
        _rwkv_post_kernel,
        out_shape=jax.ShapeDtypeStruct((T, W), BF16),
        grid=(T // tb,),
        in_specs=[spec, spec, spec, row, row],
        out_specs=spec,
        compiler_params=_params(("parallel",)),
        name="rwkv_post",
    )(y, bonus, g, ln_w.reshape(1, W), ln_b.reshape(1, W))


def rwkv_mixer(u, mu, w0, w2, a0, a2, g2, k_k, k_a, r_k, ln_w, ln_b):
    r, lw, k, v, kk, a, g, bonus = rwkv_prep(u, mu, w0, w2, a0, a2, g2, k_k, k_a, r_k)
    y = rwkv_scan(r, lw, k, v, kk, a)
    return rwkv_post(y, bonus, g, ln_w, ln_b)


def _ssd_kernel(p_ref, cw_ref, cb_ref, dtb_ref, alog_ref, dsk_ref, ng_ref, o_ref,
                ext_ref, xbc_ref, st_ref, y_scr):
    c = pl.program_id(0)
    L = SSM_CHUNK
    X0 = SSM_INNER
    HP = 2 * 64

    @pl.when(c == 0)
    def _():
        ext_ref[0:8, :] = jnp.zeros((8, SSM_CONV_DIM), F32)
        st_ref[...] = jnp.zeros_like(st_ref)

    ext_ref[8:8 + L, :] = p_ref[:, X0:X0 + SSM_CONV_DIM]
    CW = 4 * LANES
    for cc in range(SSM_CONV_DIM // CW):
        sl = slice(cc * CW, (cc + 1) * CW)
        conv = jnp.broadcast_to(cb_ref[:, sl], (L, CW))
        for kk in range(SSM_CONV):
            conv = conv + cw_ref[kk:kk + 1, sl] * ext_ref[pl.ds(8 - (SSM_CONV - 1) + kk, L), sl]
        xbc_ref[:, sl] = _silu(conv)
    ext_ref[0:8, :] = p_ref[L - 8:L, X0:X0 + SSM_CONV_DIM]

    dt = _softplus(p_ref[:, X0 + SSM_CONV_DIM:X0 + SSM_CONV_DIM + LANES] + dtb_ref[...])
    d_a = dt * (-jnp.exp(alog_ref[...]))
    ti = lax.broadcasted_iota(jnp.int32, (L, L), 0)
    tj = lax.broadcasted_iota(jnp.int32, (L, L), 1)
    causal = ti >= tj
    a_cum = _dot_exact_lhs(jnp.where(causal, 1.0, 0.0).astype(BF16), d_a)
    a_cum_t = a_cum.T
    dt_t = dt.T
    lane = lax.broadcasted_iota(jnp.int32, (L, HP), 1)
    first = lane < 64

    for g in range(SSM_GROUPS):
        b_m = xbc_ref[:,SSM_INNER + g * SSM_STATE:SSM_INNER + (g + 1) * SSM_STATE]
        c_m = xbc_ref[:,SSM_INNER + SSM_GROUPS * SSM_STATE + g * SSM_STATE:
                  SSM_INNER + SSM_GROUPS * SSM_STATE + (g + 1) * SSM_STATE]
        cb = _bdot_nt(c_m, b_m)
        b_t = b_m.T
        for j in range(3):
            pair = g * 3 + j
            xs = xbc_ref[:,pair * HP:(pair + 1) * HP]
            yd, st, ea, cd = [], [], [], []
            for hh in (2 * pair, 2 * pair + 1):
                col = a_cum[:, hh:hh + 1]
                row = a_cum_t[hh:hh + 1, :]
                dtr = dt_t[hh:hh + 1, :]
                last = a_cum[L - 1:L, hh:hh + 1]
                lmat = jnp.exp(jnp.where(causal, col - row, -jnp.inf))
                yd.append(_bdot(cb * lmat * dtr, xs))
                st.append(_bdot(b_t * (jnp.exp(last - row) * dtr), xs))
                ea.append(jnp.exp(col))
                cd.append(jnp.exp(last))
            s_in = st_ref[pair]
            y = jnp.where(first, yd[0], yd[1])
            y = y + _bdot(c_m, s_in) * jnp.where(first, ea[0], ea[1])
            st_ref[pair] = s_in * jnp.where(first, cd[0], cd[1]) + jnp.where(first, st[0], st[1])
            y_scr[:, pair * HP:(pair + 1) * HP] = y + xs * dsk_ref[:, pair * HP:(pair + 1) * HP]

    gw = SSM_INNER // SSM_GROUPS
    for g in range(SSM_GROUPS):
        sl = slice(g * gw, (g + 1) * gw)
        yg = y_scr[:, sl] * _silu(p_ref[:, sl])
        ms = jnp.mean(yg * yg, axis=-1, keepdims=True)
        o_ref[:, sl] = (yg * lax.rsqrt(ms + SSM_NORM_EPS) * ng_ref[:, sl]).astype(o_ref.dtype)


def mamba2_mixer(u, conv_w, conv_b, dt_bias, a_log, d_skip, norm_g):
    T = u.shape[0]
    L = SSM_CHUNK
    padh = lambda x: jnp.pad(x, (0, LANES - SSM_HEADS)).reshape(1, LANES)
    full = lambda a: pl.BlockSpec(a.shape, lambda i: (0,) * a.ndim)
    args = (u, conv_w, conv_b.reshape(1, -1), padh(dt_bias), padh(a_log),
            jnp.repeat(d_skip, 64).reshape(1, SSM_INNER), norm_g.reshape(1, SSM_INNER))
    return pl.pallas_call(
        _ssd_kernel,
        out_shape=jax.ShapeDtypeStruct((T, SSM_INNER), BF16),
        grid=(T // L,),
        in_specs=[pl.BlockSpec((L, SSM_PAD), lambda i: (i, 0))] + [full(a) for a in args[1:]],
        out_specs=pl.BlockSpec((L, SSM_INNER), lambda i: (i, 0)),
        scratch_shapes=[pltpu.VMEM((L + 8, SSM_CONV_DIM), F32),
                        pltpu.VMEM((L, SSM_CONV_DIM), F32),
                        pltpu.VMEM((SSM_HEADS // 2, SSM_STATE, LANES), F32),
                        pltpu.VMEM((L, SSM_INNER), F32)],
        compiler_params=_params(("arbitrary",)),
        name="ssd",
    )(*args)


def kernel(x, norm_mix_g, w_in, rwkv_mu, rwkv_w0, rwkv_w2, rwkv_a0, rwkv_a2, rwkv_g2, rwkv_k_k, rwkv_k_a, rwkv_r_k, rwkv_ln_w, rwkv_ln_b, ssm_conv_w, ssm_conv_b, ssm_dt_bias, ssm_a_log, ssm_d, ssm_norm_g, p_attn, p_rwkv, p_ssm, w_out, norm_ffn_g, w_ffn_gate, w_ffn_up, w_ffn_down, norm_final_g):
    B, T, D = x.shape
    assert B == 1 and D == D_MODEL
    xt = x.reshape(T, D)
    o_r = ATT_COLS
    o_s = ATT_COLS + RWKV_COLS
    o_g = ATT_COLS + RWKV_COLS + SSM_COLS
    TM, TN = 1024, 512
    for l in range(DEPTH):
        w_ssm = w_in[l, :, o_s:o_s + SSM_PAD].astype(BF16)
        w_gate = w_in[l, :, o_g:].astype(BF16)

        h = rms_norm(xt, norm_mix_g[l], BF16)
        qkv = matmul_ws(h, w_in, l, 0, ATT_COLS, F32, TM, TN, "proj_att")
        u_rwkv = matmul_ws(h, w_in, l, o_r, RWKV_PAD, F32, TM, TN, "proj_rwkv")
        u_ssm = matmul(h, w_ssm, F32, TM, TN, "proj_ssm")
        gates = matmul(h, w_gate, F32, TM, TN, "proj_gate")

        y_att = attention_mixer(qkv)
        g2 = jnp.pad(rwkv_g2[l], ((0, RWKV_G_PAD - RWKV_LORA_G), (0, 0)))
        mu = jnp.pad(rwkv_mu[l], (0, RWKV_PAD - RWKV_COLS))
        y_rwkv = rwkv_mixer(u_rwkv, mu, rwkv_w0[l], rwkv_w2[l], rwkv_a0[l], rwkv_a2[l], g2,
                            rwkv_k_k[l], rwkv_k_a[l], rwkv_r_k[l].reshape(-1), rwkv_ln_w[l], rwkv_ln_b[l])
        y_ssm = mamba2_mixer(u_ssm, ssm_conv_w[l], ssm_conv_b[l], ssm_dt_bias[l], ssm_a_log[l],
                             ssm_d[l], ssm_norm_g[l])

        merged = merge_proj(y_att, y_rwkv, y_ssm, p_attn[l].astype(BF16), p_rwkv[l].astype(BF16),
                            p_ssm[l].astype(BF16), gates, TM // 2, TN)
        xt = matmul_residual(merged, w_out[l].astype(BF16), xt, TM, TN, "w_out")

        h2 = rms_norm(xt, norm_ffn_g[l], BF16)
        act = ffn_up_ws(h2, w_ffn_gate, w_ffn_up, l, TM, 256)
        xt = matmul_residual(act, w_ffn_down[l].astype(BF16), xt, TM // 2, TN, "ffn_down")
    out = rms_norm(xt, norm_final_g, F32)
    return out.reshape(B, T, D)
```

```python
import functools

import jax
import jax.numpy as jnp
from jax import lax
from jax.experimental import pallas as pl
from jax.experimental.pallas import tpu as pltpu

F32 = jnp.float32
BF16 = jnp.bfloat16

D_MODEL = 4096
DEPTH = 2
NORM_EPS = 1e-6
ATT_GROUPS = ((128, 1), (512, 4), (2048, 16))
ATT_HEADS = 8
ATT_DIM = 128
ATT_WIDTH = 3072
ATT_OUT = 1024
ATT_COLS = 9216
RWKV_WIDTH = 3072
RWKV_HEAD = 64
RWKV_LORA_W = 128
RWKV_LORA_A = 128
RWKV_LORA_G = 480
RWKV_COLS = 9952
RWKV_LN_EPS = 64e-5
SSM_INNER = 3072
SSM_HEADS = 48
SSM_GROUPS = 8
SSM_STATE = 128
SSM_CONV = 4
SSM_CHUNK = 128
SSM_CONV_DIM = 5120
SSM_COLS = 8240
SSM_NORM_EPS = 1e-5
FFN_HIDDEN = 11008
GATE_COLS = 3 * D_MODEL

RWKV_PAD = 10240
RWKV_G_PAD = 512
SSM_PAD = 8704

LANES = 128
RWKV_CHUNK = 64
VMEM_LIMIT = 56 * 1024 * 1024


def _params(sem):
    return pltpu.CompilerParams(dimension_semantics=sem, vmem_limit_bytes=VMEM_LIMIT)


def _bdot(a, b):
    return jnp.dot(a.astype(BF16), b.astype(BF16), preferred_element_type=F32)


def _bdot_nt(a, b):
    return lax.dot_general(a.astype(BF16), b.astype(BF16), (((1,), (1,)), ((), ())),
                           preferred_element_type=F32)


def _bdot_tn(a, b):
    return lax.dot_general(a.astype(BF16), b.astype(BF16), (((0,), (0,)), ((), ())),
                           preferred_element_type=F32)


def _split2(x):
    hi = x.astype(BF16)
    lo = (x - hi.astype(F32)).astype(BF16)
    return hi, lo


def _split3(x):
    hi = x.astype(BF16)
    r1 = x - hi.astype(F32)
    mid = r1.astype(BF16)
    lo = (r1 - mid.astype(F32)).astype(BF16)
    return hi, mid, lo


def _dot_exact_lhs(m_bf16, x):
    hi, mid, lo = _split3(x)
    d = functools.partial(jnp.dot, preferred_element_type=F32)
    return d(m_bf16, hi) + d(m_bf16, mid) + d(m_bf16, lo)


def _dot_exact_rhs(x, m_bf16):
    hi, mid, lo = _split3(x)
    d = functools.partial(jnp.dot, preferred_element_type=F32)
    return d(hi, m_bf16) + d(mid, m_bf16) + d(lo, m_bf16)


def _dot_hi(a, b):
    ah, al = _split2(a)
    bh, bl = _split2(b)
    d = functools.partial(jnp.dot, preferred_element_type=F32)
    return d(ah, bh) + d(ah, bl) + d(al, bh)


def _sigmoid(x):
    return 1.0 / (1.0 + jnp.exp(-x))


def _silu(x):
    return x * _sigmoid(x)


def _softplus(x):
    return jnp.maximum(x, 0.0) + jnp.log(1.0 + jnp.exp(-jnp.abs(x)))


def _rms_kernel(x_ref, g_ref, o_ref):
    x = x_ref[...]
    ms = jnp.mean(x * x, axis=-1, keepdims=True)
    o_ref[...] = (x * lax.rsqrt(ms + NORM_EPS) * g_ref[...]).astype(o_ref.dtype)


def rms_norm(x, g, out_dtype, tr=256):
    T, D = x.shape
    return pl.pallas_call(
        _rms_kernel,
        out_shape=jax.ShapeDtypeStruct((T, D), out_dtype),
        grid=(T // tr,),
        in_specs=[pl.BlockSpec((tr, D), lambda i: (i, 0)),
                  pl.BlockSpec((1, D), lambda i: (0, 0))],
        out_specs=pl.BlockSpec((tr, D), lambda i: (i, 0)),
        compiler_params=_params(("parallel",)),
        name="rms_norm",
    )(x, g.reshape(1, D))


def _mm_nt_kernel(a_ref, w_ref, o_ref):
    o_ref[...] = lax.dot_general(a_ref[...], w_ref[0], (((1,), (1,)), ((), ())),
                                 preferred_element_type=F32).astype(o_ref.dtype)


def matmul_nt(a, wt, layer, row0, n_out, out_dtype, tm, tn, name):
    M, K = a.shape
    assert row0 % 16 == 0
    return pl.pallas_call(
        _mm_nt_kernel,
        out_shape=jax.ShapeDtypeStruct((M, n_out), out_dtype),
        grid=(M // tm, n_out // tn),
        in_specs=[pl.BlockSpec((tm, K), lambda i, j: (i, 0)),
                  pl.BlockSpec((pl.Element(1), pl.Element(tn), pl.Element(K)),
                               lambda i, j: (layer, pl.multiple_of(row0 + j * tn, 16), 0))],
        out_specs=pl.BlockSpec((tm, tn), lambda i, j: (i, j)),
        compiler_params=_params(("parallel", "arbitrary")),
        name=name,
    )(a, wt)


def _w_spec(w, layer, tn, col):
    if w.ndim == 2:
        return pl.BlockSpec((w.shape[0], tn), lambda i, j: (0, col(j)))
    return pl.BlockSpec((None, w.shape[1], tn), lambda i, j: (layer, 0, col(j)))


def _ffn_up_ws_kernel(h_ref, wg_ref, wu_ref, o_ref, wgb_ref, wub_ref):
    @pl.when(pl.program_id(1) == 0)
    def _():
        wgb_ref[...] = wg_ref[...].astype(BF16)
        wub_ref[...] = wu_ref[...].astype(BF16)

    h = h_ref[...]
    gate = jnp.dot(h, wgb_ref[...], preferred_element_type=F32)
    up = jnp.dot(h, wub_ref[...], preferred_element_type=F32)
    o_ref[...] = (_silu(gate) * up).astype(o_ref.dtype)


def ffn_up_ws(h, wg_stack, wu_stack, layer, tm, tn):
    M, K = h.shape
    N = wg_stack.shape[2]
    wspec = pl.BlockSpec((None, K, tn), lambda j, i: (layer, 0, j))
    return pl.pallas_call(
        _ffn_up_ws_kernel,
        out_shape=jax.ShapeDtypeStruct((M, N), BF16),
        grid=(N // tn, M // tm),
        in_specs=[pl.BlockSpec((tm, K), lambda j, i: (i, 0)), wspec, wspec],
        out_specs=pl.BlockSpec((tm, tn), lambda j, i: (i, j)),
        scratch_shapes=[pltpu.VMEM((K, tn), BF16)] * 2,
        compiler_params=_params(("arbitrary", "arbitrary")),
        name="ffn_up",
    )(h, wg_stack, wu_stack)


def _mm_res_kernel(a_ref, w_ref, x_ref, o_ref):
    o_ref[...] = x_ref[...] + jnp.dot(a_ref[...], w_ref[...], preferred_element_type=F32)


def matmul_residual(a, w, x, tm, tn, name, layer=0):
    M, K = a.shape
    N = w.shape[-1]
    return pl.pallas_call(
        _mm_res_kernel,
        out_shape=jax.ShapeDtypeStruct((M, N), F32),
        grid=(M // tm, N // tn),
        in_specs=[pl.BlockSpec((tm, K), lambda i, j: (i, 0)),
                  _w_spec(w, layer, tn, lambda j: j),
                  pl.BlockSpec((tm, tn), lambda i, j: (i, j))],
        out_specs=pl.BlockSpec((tm, tn), lambda i, j: (i, j)),
        compiler_params=_params(("parallel", "arbitrary")),
        name=name,
    )(a, w, x)


def _merge_proj_kernel(ya_ref, yr_ref, ys_ref, pa_ref, pr_ref, ps_ref,
                       ga_ref, gr_ref, gs_ref, o_ref):
    acc = _sigmoid(ga_ref[...]) * jnp.dot(ya_ref[...], pa_ref[...], preferred_element_type=F32)
    acc += _sigmoid(gr_ref[...]) * jnp.dot(yr_ref[...], pr_ref[...], preferred_element_type=F32)
    acc += _sigmoid(gs_ref[...]) * jnp.dot(ys_ref[...], ps_ref[...], preferred_element_type=F32)
    o_ref[...] = acc.astype(o_ref.dtype)


def merge_proj(y_att, y_rwkv, y_ssm, p_att, p_rwkv, p_ssm, gates, tm, tn, layer):
    M = y_att.shape[0]
    N = p_att.shape[-1]
    nb = N // tn
    return pl.pallas_call(
        _merge_proj_kernel,
        out_shape=jax.ShapeDtypeStruct((M, N), BF16),
        grid=(M // tm, nb),
        in_specs=[pl.BlockSpec((tm, y_att.shape[1]), lambda i, j: (i, 0)),
                  pl.BlockSpec((tm, y_rwkv.shape[1]), lambda i, j: (i, 0)),
                  pl.BlockSpec((tm, y_ssm.shape[1]), lambda i, j: (i, 0)),
                  _w_spec(p_att, layer, tn, lambda j: j),
                  _w_spec(p_rwkv, layer, tn, lambda j: j),
                  _w_spec(p_ssm, layer, tn, lambda j: j),
                  pl.BlockSpec((tm, tn), lambda i, j: (i, j)),
                  pl.BlockSpec((tm, tn), lambda i, j: (i, j + nb)),
                  pl.BlockSpec((tm, tn), lambda i, j: (i, j + 2 * nb))],
        out_specs=pl.BlockSpec((tm, tn), lambda i, j: (i, j)),
        compiler_params=_params(("parallel", "arbitrary")),
        name="merge_proj",
    )(y_att, y_rwkv, y_ssm, p_att, p_rwkv, p_ssm, gates, gates, gates)


ATT_WIN = 2048
ATT_UNROLL = 4


def _rows(start, size, stride):
    return pl.ds(start, size) if stride == 1 else pl.ds(start, size, stride=stride)


def _attn_kernel(*refs):
    n_g = len(ATT_GROUPS)
    ins = refs[:5 * n_g]
    y_ref = refs[5 * n_g]
    scr = refs[5 * n_g + 1:]
    kext, vext = scr[0:n_g], scr[n_g:2 * n_g]
    o_scr, l_scr = scr[2 * n_g], scr[2 * n_g + 1]
    w = pl.program_id(1)
    blk = ATT_DIM
    qi = lax.broadcasted_iota(jnp.int32, (blk, 2 * blk), 0)
    kj = lax.broadcasted_iota(jnp.int32, (blk, 2 * blk), 1)
    dist = qi + blk - kj
    scale = ATT_DIM ** -0.5

    for g, (window, d) in enumerate(ATT_GROUPS):
        q_ref, k_ref, v_ref, kp_ref, vp_ref = ins[5 * g:5 * g + 5]
        halo = d * blk
        kext[g][0:halo, :] = kp_ref[...]
        kext[g][halo:halo + ATT_WIN, :] = k_ref[...]
        vext[g][0:halo, :] = vp_ref[...]
        vext[g][halo:halo + ATT_WIN, :] = v_ref[...]
        band = (dist >= 0) & (dist <= window // d)
        shift = d.bit_length() - 1

        def body(it, carry, g=g, d=d, q_ref=q_ref, band=band, shift=shift, halo=halo):
            for u in range(ATT_UNROLL):
                idx = it * ATT_UNROLL + u
                res = jnp.bitwise_and(idx, d - 1)
                m = jnp.right_shift(idx, shift)
                row0 = res + halo * m
                q = q_ref[_rows(row0, blk, d), :].astype(BF16)
                k = kext[g][_rows(row0, 2 * blk, d), :].astype(BF16)
                v = vext[g][_rows(row0, 2 * blk, d), :].astype(BF16)
                s = lax.dot_general(q, k, (((1,), (1,)), ((), ())), preferred_element_type=F32) * scale
                has_prev = jnp.logical_or(w > 0, m > 0)
                s = jnp.where(band & jnp.logical_or(has_prev, kj >= blk), s, -jnp.inf)
                mx = jnp.max(s, axis=-1, keepdims=True)
                p = jnp.exp(s - mx)
                den = jnp.sum(p, axis=-1, keepdims=True)
                o = jnp.dot((p / den).astype(BF16), v, preferred_element_type=F32)
                o_scr[g, _rows(row0, blk, d), :] = o
                l_scr[g, _rows(row0, blk, d), :] = jnp.broadcast_to(mx + jnp.log(den), (blk, blk))
            return carry

        lax.fori_loop(0, ATT_WIN // (blk * ATT_UNROLL), body, 0)

    rc = 256
    for c in range(ATT_WIN // rc):
        rows = slice(c * rc, (c + 1) * rc)
        la, lb, lc = l_scr[0, rows, :], l_scr[1, rows, :], l_scr[2, rows, :]
        mx = jnp.maximum(jnp.maximum(la, lb), lc)
        ea, eb, ec = jnp.exp(la - mx), jnp.exp(lb - mx), jnp.exp(lc - mx)
        y = (ea * o_scr[0, rows, :] + eb * o_scr[1, rows, :] + ec * o_scr[2, rows, :]) / (ea + eb + ec)
        y_ref[rows, :] = y.astype(y_ref.dtype)


def attention_mixer(qkv):
    T = qkv.shape[0]
    blk = ATT_DIM
    n_g = len(ATT_GROUPS)
    sect = n_g * ATT_HEADS
    in_specs, scratch_k, scratch_v = [], [], []
    for g, (window, d) in enumerate(ATT_GROUPS):
        per_win = ATT_WIN // (d * blk)
        for s in range(3):
            in_specs.append(pl.BlockSpec((ATT_WIN, blk),
                                         lambda h, w, s=s, g=g: (w, s * sect + g * ATT_HEADS + h)))
        for s in (1, 2):
            in_specs.append(pl.BlockSpec(
                (d * blk, blk),
                lambda h, w, s=s, g=g, per_win=per_win: (jnp.maximum(w * per_win - 1, 0),
                                                         s * sect + g * ATT_HEADS + h)))
        scratch_k.append(pltpu.VMEM((ATT_WIN + d * blk, blk), F32))
        scratch_v.append(pltpu.VMEM((ATT_WIN + d * blk, blk), F32))
    return pl.pallas_call(
        _attn_kernel,
        out_shape=jax.ShapeDtypeStruct((T, ATT_OUT), BF16),
        grid=(ATT_HEADS, T // ATT_WIN),
        in_specs=in_specs,
        out_specs=pl.BlockSpec((ATT_WIN, blk), lambda h, w: (w, h)),
        scratch_shapes=scratch_k + scratch_v + [pltpu.VMEM((n_g, ATT_WIN, blk), F32)] * 2,
        compiler_params=_params(("parallel", "arbitrary")),
        name="attention",
    )(*([qkv] * (5 * n_g)))


def _head_sum_matrix():
    i = lax.broadcasted_iota(jnp.int32, (LANES, LANES), 0) // RWKV_HEAD
    j = lax.broadcasted_iota(jnp.int32, (LANES, LANES), 1) // RWKV_HEAD
    return jnp.where(i == j, 1.0, 0.0).astype(BF16)


def _head_sums(x, bd):
    cols = []
    for c in range(x.shape[1] // LANES):
        cols.append(_dot_exact_rhs(x[:, c * LANES:(c + 1) * LANES], bd))
    return jnp.concatenate(cols, axis=1)


def _rwkv_prep_kernel(u_ref, mu_ref, w0_ref, w2_ref, a0_ref, a2_ref, g2_ref, kk_ref, ka_ref, rk_ref,
                      r_out, lw_out, k_out, v_out, kkn_out, a_out, g_out, bonus_out, carry_ref):
    i = pl.program_id(0)
    tb = u_ref.shape[0]
    W = RWKV_WIDTH
    CW = 2 * LANES

    @pl.when(i == 0)
    def _():
        carry_ref[...] = jnp.zeros_like(carry_ref)

    def mixed(lo, width):
        sl = slice(lo, lo + width)
        u = u_ref[:, sl]
        row = lax.broadcasted_iota(jnp.int32, u.shape, 0)
        u_prev = jnp.where(row == 0, jnp.broadcast_to(carry_ref[0:1, sl], u.shape), pltpu.roll(u, 1, 0))
        return u + (u_prev - u) * mu_ref[:, sl]

    o = 3 * W
    th = jnp.tanh(mixed(o, RWKV_LORA_W))
    xa = mixed(o + RWKV_LORA_W, RWKV_LORA_A)
    sg = _sigmoid(mixed(o + RWKV_LORA_W + RWKV_LORA_A, RWKV_G_PAD))
    bd = _head_sum_matrix()
    for cc in range(W // CW):
        sl = slice(cc * CW, (cc + 1) * CW)
        r = mixed(cc * CW, CW)
        k = mixed(W + cc * CW, CW)
        v = mixed(2 * W + cc * CW, CW)
        w_log = -_softplus(-(w0_ref[:, sl] + _dot_hi(th, w2_ref[:, sl]))) - 0.5
        lw_out[:, sl] = -jnp.exp(w_log)
        a = _sigmoid(a0_ref[:, sl] + _dot_hi(xa, a2_ref[:, sl]))
        g_out[:, sl] = _dot_hi(sg, g2_ref[:, sl])
        kk = k * kk_ref[:, sl]
        kk = kk / jnp.maximum(jnp.sqrt(_head_sums(kk * kk, bd)), 1e-12)
        k2 = k * (1.0 + (a - 1.0) * ka_ref[:, sl])
        bonus_out[:, sl] = _head_sums(r * k2 * rk_ref[:, sl], bd) * v
        r_out[:, sl] = r
        k_out[:, sl] = k2
        v_out[:, sl] = v
        kkn_out[:, sl] = kk
        a_out[:, sl] = a
    carry_ref[0:1, :] = u_ref[tb - 1:tb, :]


def rwkv_prep(u, mu, w0, w2, a0, a2, g2, k_k, k_a, r_k, tb=128):
    T = u.shape[0]
    W = RWKV_WIDTH
    row = lambda n: pl.BlockSpec((1, n), lambda i: (0, 0))
    full = lambda a: pl.BlockSpec(a.shape, lambda i: (0, 0))
    out_spec = pl.BlockSpec((tb, W), lambda i: (i, 0))
    return pl.pallas_call(
        _rwkv_prep_kernel,
        out_shape=[jax.ShapeDtypeStruct((T, W), F32)] * 8,
        grid=(T // tb,),
        in_specs=[pl.BlockSpec((tb, RWKV_PAD), lambda i: (i, 0)), row(RWKV_PAD), row(W), full(w2),
                  row(W), full(a2), full(g2), row(W), row(W), row(W)],
        out_specs=[out_spec] * 8,
        scratch_shapes=[pltpu.VMEM((8, RWKV_PAD), F32)],
        compiler_params=_params(("arbitrary",)),
        name="rwkv_prep",
    )(u, mu.reshape(1, -1), w0.reshape(1, W), w2, a0.reshape(1, W), a2, g2,
      k_k.reshape(1, W), k_a.reshape(1, W), r_k.reshape(1, W))


def _rwkv_chunk(ins, states, consts):
    tril_l, m0, strict, incl, eye = consts
    P = range(len(ins))
    L = ins[0][0].shape[0]
    n2 = 2 * L

    def stack(x):
        return jnp.concatenate([jnp.where(m0, x, 0.0), jnp.where(m0, 0.0, x)], axis=0)

    c = [_dot_exact_lhs(tril_l, ins[p][1]) for p in P]
    c_last = [c[p][L - 1:L, :] for p in P]
    lhs, rhs, r_s, a_s, v_s, lhs2 = [], [], [], [], [], []
    for p in P:
        r, lw, k, v, kk, a = ins[p]
        n_in = jnp.exp(-c[p])
        to_end = jnp.exp(c_last[p] - c[p])
        a_t = -kk * jnp.exp(c[p] - lw)
        b_raw = kk * a
        b_t = b_raw * n_in
        k_t = k * n_in
        a_s.append(stack(a_t))
        r_s.append(stack(r * jnp.exp(c[p])))
        v_s.append(stack(v))
        lhs.append(jnp.concatenate([a_s[p], r_s[p]], axis=0))
        rhs.append(jnp.concatenate([b_t, b_t, k_t, k_t], axis=0))
        lhs2.append(jnp.concatenate([stack(b_raw * to_end), stack(k * to_end)], axis=0))
    big = [_bdot_nt(lhs[p], rhs[p]) for p in P]
    a_ab = [jnp.where(strict, big[p][0:n2, 0:n2], 0.0) for p in P]
    a_ak = [jnp.where(strict, big[p][0:n2, n2:2 * n2], 0.0) for p in P]
    m_rb = [jnp.where(incl, big[p][n2:2 * n2, 0:n2], 0.0) for p in P]
    m_rk = [jnp.where(incl, big[p][n2:2 * n2, n2:2 * n2], 0.0) for p in P]

    t_inv = [eye + a_ab[p] for p in P]
    pw = [_bdot(a_ab[p], a_ab[p]) for p in P]
    av = [_bdot(a_ak[p], v_s[p]) for p in P]
    steps = (L - 1).bit_length() - 1
    for s in range(steps):
        if s < steps - 1:
            res = [_bdot(jnp.concatenate([t_inv[p], pw[p]], axis=0), pw[p]) for p in P]
            t_inv = [t_inv[p] + res[p][0:n2] for p in P]
            pw = [res[p][n2:2 * n2] for p in P]
        else:
            res = [_bdot(t_inv[p], pw[p]) for p in P]
            t_inv = [t_inv[p] + res[p] for p in P]

    sol = [_bdot(t_inv[p], jnp.concatenate([a_s[p], av[p]], axis=1)) for p in P]
    qy = [_bdot(m_rb[p], sol[p]) for p in P]
    yv = [_bdot(m_rk[p], v_s[p]) for p in P]
    gh = [_bdot_tn(lhs2[p], jnp.concatenate(
        [sol[p], jnp.concatenate([jnp.zeros_like(v_s[p]), v_s[p]], axis=1)], axis=0)) for p in P]
    outs = []
    for p in P:
        q_s = r_s[p] + qy[p][:, 0:LANES]
        y_s = qy[p][:, LANES:2 * LANES] + yv[p]
        q_h = q_s[0:L] + q_s[L:n2]
        y_in = y_s[0:L] + y_s[L:n2]
        g_mat = gh[p][:, 0:LANES] + jnp.where(
            eye > 0, jnp.broadcast_to(jnp.exp(c_last[p]), (LANES, LANES)), 0.0)
        y = _bdot(q_h, states[p]) + y_in
        s_new = _bdot(g_mat, states[p]) + gh[p][:, LANES:2 * LANES]
        outs.append((y, s_new))
    return outs


def _rwkv_scan_kernel(r_ref, lw_ref, k_ref, v_ref, kk_ref, a_ref, y_ref, s_ref, *, pairs):
    c = pl.program_id(1)
    L = r_ref.shape[0]
    n2 = 2 * L

    @pl.when(c == 0)
    def _():
        s_ref[...] = jnp.zeros_like(s_ref)

    ti = lax.broadcasted_iota(jnp.int32, (L, L), 0)
    tj = lax.broadcasted_iota(jnp.int32, (L, L), 1)
    tril_l = jnp.where(ti >= tj, 1.0, 0.0).astype(BF16)
    m0 = lax.broadcasted_iota(jnp.int32, (L, LANES), 1) < RWKV_HEAD
    si = lax.broadcasted_iota(jnp.int32, (n2, n2), 0)
    sj = lax.broadcasted_iota(jnp.int32, (n2, n2), 1)
    same = (si // L) == (sj // L)
    strict = same & ((sj % L) < (si % L))
    incl = same & ((sj % L) <= (si % L))
    eye = jnp.where(si == sj, 1.0, 0.0).astype(F32)
    consts = (tril_l, m0, strict, incl, eye)

    sls = [slice(p * LANES, (p + 1) * LANES) for p in range(pairs)]
    ins = [(r_ref[:, sl], lw_ref[:, sl], k_ref[:, sl], v_ref[:, sl], kk_ref[:, sl], a_ref[:, sl])
           for sl in sls]
    outs = _rwkv_chunk(ins, [s_ref[p] for p in range(pairs)], consts)
    for p in range(pairs):
        y_ref[:, sls[p]] = outs[p][0]
        s_ref[p] = outs[p][1]


def rwkv_scan(r, lw, k, v, kk, a, pairs=8):
    T, W = r.shape
    L = RWKV_CHUNK
    wb = pairs * LANES
    spec = pl.BlockSpec((L, wb), lambda p, c: (c, p))
    return pl.pallas_call(
        functools.partial(_rwkv_scan_kernel, pairs=pairs),
        out_shape=jax.ShapeDtypeStruct((T, W), F32),
        grid=(W // wb, T // L),
        in_specs=[spec] * 6,
        out_specs=spec,
        scratch_shapes=[pltpu.VMEM((pairs, LANES, LANES), F32)],
        compiler_params=_params(("parallel", "arbitrary")),
        name="rwkv_scan",
    )(r, lw, k, v, kk, a)


def _rwkv_post_kernel(y_ref, bonus_ref, g_ref, lnw_ref, lnb_ref, o_ref):
    bd = _head_sum_matrix()
    y = y_ref[...]
    inv_n = 1.0 / RWKV_HEAD
    mean = _head_sums(y, bd) * inv_n
    yc = y - mean
    var = _head_sums(yc * yc, bd) * inv_n
    out = yc * lax.rsqrt(var + RWKV_LN_EPS) * lnw_ref[...] + lnb_ref[...] + bonus_ref[...]
    o_ref[...] = (out * g_ref[...]).astype(o_ref.dtype)


def rwkv_post(y, bonus, g, ln_w, ln_b, tb=256):
    T, W = y.shape
    spec = pl.BlockSpec((tb, W), lambda i: (i, 0))
    row = pl.BlockSpec((1, W), lambda i: (0, 0))
    return pl.pallas_call(
        _rwkv_post_kernel,
        out_shape=jax.ShapeDtypeStruct((T, W), BF16),
        grid=(T // tb,),
        in_specs=[spec, spec, spec, row, row],
        out_specs=spec,
        compiler_params=_params(("parallel",)),
        name="rwkv_post",
    )(y, bonus, g, ln_w.reshape(1, W), ln_b.reshape(1, W))


def rwkv_mixer(u, mu, w0, w2, a0, a2, g2, k_k, k_a, r_k, ln_w, ln_b):
    r, lw, k, v, kk, a, g, bonus = rwkv_prep(u, mu, w0, w2, a0, a2, g2, k_k, k_a, r_k)
    y = rwkv_scan(r, lw, k, v, kk, a)
    return rwkv_post(y, bonus, g, ln_w, ln_b)


def _ssd_kernel(p_ref, cw_ref, cb_ref, dtb_ref, alog_ref, dsk_ref, ng_ref, o_ref,
                ext_ref, xbc_ref, st_ref, y_scr):
    c = pl.program_id(0)
    L = SSM_CHUNK
    X0 = SSM_INNER
    HP = 2 * 64

    @pl.when(c == 0)
    def _():
        ext_ref[0:8, :] = jnp.zeros((8, SSM_CONV_DIM), F32)
        st_ref[...] = jnp.zeros_like(st_ref)

    ext_ref[8:8 + L, :] = p_ref[:, X0:X0 + SSM_CONV_DIM]
    CW = 4 * LANES
    for cc in range(SSM_CONV_DIM // CW):
        sl = slice(cc * CW, (cc + 1) * CW)
        conv = jnp.broadcast_to(cb_ref[:, sl], (L, CW))
        for kk in range(SSM_CONV):
            conv = conv + cw_ref[kk:kk + 1, sl] * ext_ref[pl.ds(8 - (SSM_CONV - 1) + kk, L), sl]
        xbc_ref[:, sl] = _silu(conv)
    ext_ref[0:8, :] = p_ref[L - 8:L, X0:X0 + SSM_CONV_DIM]

    dt = _softplus(p_ref[:, X0 + SSM_CONV_DIM:X0 + SSM_CONV_DIM + LANES] + dtb_ref[...])
    d_a = dt * (-jnp.exp(alog_ref[...]))
    ti = lax.broadcasted_iota(jnp.int32, (L, L), 0)
    tj = lax.broadcasted_iota(jnp.int32, (L, L), 1)
    causal = ti >= tj
    a_cum = _dot_exact_lhs(jnp.where(causal, 1.0, 0.0).astype(BF16), d_a)
    a_cum_t = a_cum.T
    dt_t = dt.T
    lane = lax.broadcasted_iota(jnp.int32, (L, HP), 1)
    first = lane < 64

    for g in range(SSM_GROUPS):
        b_m = xbc_ref[:,SSM_INNER + g * SSM_STATE:SSM_INNER + (g + 1) * SSM_STATE]
        c_m = xbc_ref[:,SSM_INNER + SSM_GROUPS * SSM_STATE + g * SSM_STATE:
                  SSM_INNER + SSM_GROUPS * SSM_STATE + (g + 1) * SSM_STATE]
        cb = _bdot_nt(c_m, b_m)
        b_t = b_m.T
        for j in range(3):
            pair = g * 3 + j
            xs = xbc_ref[:,pair * HP:(pair + 1) * HP]
            yd, st, ea, cd = [], [], [], []
            for hh in (2 * pair, 2 * pair + 1):
                col = a_cum[:, hh:hh + 1]
                row = a_cum_t[hh:hh + 1, :]
                dtr = dt_t[hh:hh + 1, :]
                last = a_cum[L - 1:L, hh:hh + 1]
                lmat = jnp.exp(jnp.where(causal, col - row, -jnp.inf))
                yd.append(_bdot(cb * lmat * dtr, xs))
                st.append(_bdot(b_t * (jnp.exp(last - row) * dtr), xs))
                ea.append(jnp.exp(col))
                cd.append(jnp.exp(last))
            s_in = st_ref[pair]
            y = jnp.where(first, yd[0], yd[1])
            y = y + _bdot(c_m, s_in) * jnp.where(first, ea[0], ea[1])
            st_ref[pair] = s_in * jnp.where(first, cd[0], cd[1]) + jnp.where(first, st[0], st[1])
            y_scr[:, pair * HP:(pair + 1) * HP] = y + xs * dsk_ref[:, pair * HP:(pair + 1) * HP]

    gw = SSM_INNER // SSM_GROUPS
    for g in range(SSM_GROUPS):
        sl = slice(g * gw, (g + 1) * gw)
        yg = y_scr[:, sl] * _silu(p_ref[:, sl])
        ms = jnp.mean(yg * yg, axis=-1, keepdims=True)
        o_ref[:, sl] = (yg * lax.rsqrt(ms + SSM_NORM_EPS) * ng_ref[:, sl]).astype(o_ref.dtype)


def mamba2_mixer(u, conv_w, conv_b, dt_bias, a_log, d_skip, norm_g):
    T = u.shape[0]
    L = SSM_CHUNK
    padh = lambda x: jnp.pad(x, (0, LANES - SSM_HEADS)).reshape(1, LANES)
    full = lambda a: pl.BlockSpec(a.shape, lambda i: (0,) * a.ndim)
    args = (u, conv_w, conv_b.reshape(1, -1), padh(dt_bias), padh(a_log),
            jnp.repeat(d_skip, 64).reshape(1, SSM_INNER), norm_g.reshape(1, SSM_INNER))
    return pl.pallas_call(
        _ssd_kernel,
        out_shape=jax.ShapeDtypeStruct((T, SSM_INNER), BF16),
        grid=(T // L,),
        in_specs=[pl.BlockSpec((L, SSM_PAD), lambda i: (i, 0))] + [full(a) for a in args[1:]],
        out_specs=pl.BlockSpec((L, SSM_INNER), lambda i: (i, 0)),
        scratch_shapes=[pltpu.VMEM((L + 8, SSM_CONV_DIM), F32),
                        pltpu.VMEM((L, SSM_CONV_DIM), F32),
                        pltpu.VMEM((SSM_HEADS // 2, SSM_STATE, LANES), F32),
                        pltpu.VMEM((L, SSM_INNER), F32)],
        compiler_params=_params(("arbitrary",)),
        name="ssd",
    )(*args)


def kernel(x, norm_mix_g, w_in, rwkv_mu, rwkv_w0, rwkv_w2, rwkv_a0, rwkv_a2, rwkv_g2, rwkv_k_k, rwkv_k_a, rwkv_r_k, rwkv_ln_w, rwkv_ln_b, ssm_conv_w, ssm_conv_b, ssm_dt_bias, ssm_a_log, ssm_d, ssm_norm_g, p_attn, p_rwkv, p_ssm, w_out, norm_ffn_g, w_ffn_gate, w_ffn_up, w_ffn_down, norm_final_g):
    B, T, D = x.shape
    assert B == 1 and D == D_MODEL
    xt = x.reshape(T, D)
    o_r = ATT_COLS
    o_s = ATT_COLS + RWKV_COLS
    o_g = ATT_COLS + RWKV_COLS + SSM_COLS
    TM, TN = 1024, 512
    p_attn_b, p_rwkv_b, p_ssm_b = p_attn.astype(BF16), p_rwkv.astype(BF16), p_ssm.astype(BF16)
    w_out_b, w_down_b = w_out.astype(BF16), w_ffn_down.astype(BF16)
    wt_in = jnp.swapaxes(w_in, 1, 2).astype(BF16)
    for l in range(DEPTH):
        h = rms_norm(xt, norm_mix_g[l], BF16)
        qkv = matmul_nt(h, wt_in, l, 0, ATT_COLS, F32, TM, TN, "proj_att")
        u_rwkv = matmul_nt(h, wt_in, l, o_r, RWKV_PAD, F32, TM, TN, "proj_rwkv")
        u_ssm = matmul_nt(h, wt_in, l, o_s, SSM_PAD, F32, TM, TN, "proj_ssm")
        gates = matmul_nt(h, wt_in, l, o_g, GATE_COLS, F32, TM, TN, "proj_gate")

        y_att = attention_mixer(qkv)
        g2 = jnp.pad(rwkv_g2[l], ((0, RWKV_G_PAD - RWKV_LORA_G), (0, 0)))
        mu = jnp.pad(rwkv_mu[l], (0, RWKV_PAD - RWKV_COLS))
        y_rwkv = rwkv_mixer(u_rwkv, mu, rwkv_w0[l], rwkv_w2[l], rwkv_a0[l], rwkv_a2[l], g2,
                            rwkv_k_k[l], rwkv_k_a[l], rwkv_r_k[l].reshape(-1), rwkv_ln_w[l], rwkv_ln_b[l])
        y_ssm = mamba2_mixer(u_ssm, ssm_conv_w[l], ssm_conv_b[l], ssm_dt_bias[l], ssm_a_log[l],
                             ssm_d[l], ssm_norm_g[l])

        merged = merge_proj(y_att, y_rwkv, y_ssm, p_attn_b, p_rwkv_b, p_ssm_b, gates, TM // 2, TN, l)
        xt = matmul_residual(merged, w_out_b, xt, TM, TN, "w_out", l)

        h2 = rms_norm(xt, norm_ffn_g[l], BF16)
        act = ffn_up_ws(h2, w_ffn_gate, w_ffn_up, l, TM, 256)
        xt = matmul_residual(act, w_down_b, xt, TM // 2, TN, "ffn_down", l)
    out = rms_norm(xt, norm_final_g, F32)
    return out.reshape(B, T, D)
```

```python
import functools

import jax
import jax.numpy as jnp
from jax import lax
from jax.experimental import pallas as pl
from jax.experimental.pallas import tpu as pltpu

F32 = jnp.float32
BF16 = jnp.bfloat16

D_MODEL = 4096
DEPTH = 2
NORM_EPS = 1e-6
ATT_GROUPS = ((128, 1), (512, 4), (2048, 16))
ATT_HEADS = 8
ATT_DIM = 128
ATT_WIDTH = 3072
ATT_OUT = 1024
ATT_COLS = 9216
RWKV_WIDTH = 3072
RWKV_HEAD = 64
RWKV_LORA_W = 128
RWKV_LORA_A = 128
RWKV_LORA_G = 480
RWKV_COLS = 9952
RWKV_LN_EPS = 64e-5
SSM_INNER = 3072
SSM_HEADS = 48
SSM_GROUPS = 8
SSM_STATE = 128
SSM_CONV = 4
SSM_CHUNK = 128
SSM_CONV_DIM = 5120
SSM_COLS = 8240
SSM_NORM_EPS = 1e-5
FFN_HIDDEN = 11008
GATE_COLS = 3 * D_MODEL

RWKV_PAD = 10240
RWKV_G_PAD = 512
SSM_PAD = 8704

LANES = 128
RWKV_CHUNK = 64
VMEM_LIMIT = 56 * 1024 * 1024


def _params(sem):
    return pltpu.CompilerParams(dimension_semantics=sem, vmem_limit_bytes=VMEM_LIMIT)


def _bdot(a, b):
    return jnp.dot(a.astype(BF16), b.astype(BF16), preferred_element_type=F32)


def _bdot_nt(a, b):
    return lax.dot_general(a.astype(BF16), b.astype(BF16), (((1,), (1,)), ((), ())),
                           preferred_element_type=F32)


def _bdot_tn(a, b):
    return lax.dot_general(a.astype(BF16), b.astype(BF16), (((0,), (0,)), ((), ())),
                           preferred_element_type=F32)


def _split2(x):
    hi = x.astype(BF16)
    lo = (x - hi.astype(F32)).astype(BF16)
    return hi, lo


def _split3(x):
    hi = x.astype(BF16)
    r1 = x - hi.astype(F32)
    mid = r1.astype(BF16)
    lo = (r1 - mid.astype(F32)).astype(BF16)
    return hi, mid, lo


def _dot_exact_lhs(m_bf16, x):
    hi, mid, lo = _split3(x)
    d = functools.partial(jnp.dot, preferred_element_type=F32)
    return d(m_bf16, hi) + d(m_bf16, mid) + d(m_bf16, lo)


def _dot_exact_rhs(x, m_bf16):
    hi, mid, lo = _split3(x)
    d = functools.partial(jnp.dot, preferred_element_type=F32)
    return d(hi, m_bf16) + d(mid, m_bf16) + d(lo, m_bf16)


def _dot_hi(a, b):
    ah, al = _split2(a)
    bh, bl = _split2(b)
    d = functools.partial(jnp.dot, preferred_element_type=F32)
    return d(ah, bh) + d(ah, bl) + d(al, bh)


def _sigmoid(x):
    return 1.0 / (1.0 + jnp.exp(-x))


def _silu(x):
    return x * _sigmoid(x)


def _softplus(x):
    return jnp.maximum(x, 0.0) + jnp.log(1.0 + jnp.exp(-jnp.abs(x)))


def _rms_kernel(x_ref, g_ref, o_ref):
    x = x_ref[...]
    ms = jnp.mean(x * x, axis=-1, keepdims=True)
    o_ref[...] = (x * lax.rsqrt(ms + NORM_EPS) * g_ref[...]).astype(o_ref.dtype)


def rms_norm(x, g, out_dtype, tr=256):
    T, D = x.shape
    return pl.pallas_call(
        _rms_kernel,
        out_shape=jax.ShapeDtypeStruct((T, D), out_dtype),
        grid=(T // tr,),
        in_specs=[pl.BlockSpec((tr, D), lambda i: (i, 0)),
                  pl.BlockSpec((1, D), lambda i: (0, 0))],
        out_specs=pl.BlockSpec((tr, D), lambda i: (i, 0)),
        compiler_params=_params(("parallel",)),
        name="rms_norm",
    )(x, g.reshape(1, D))


def _mm_nt_kernel(a_ref, w_ref, o_ref):
    o_ref[...] = lax.dot_general(a_ref[...], w_ref[0], (((1,), (1,)), ((), ())),
                                 preferred_element_type=F32).astype(o_ref.dtype)


def matmul_nt(a, wt, layer, row0, n_out, out_dtype, tm, tn, name):
    M, K = a.shape
    assert row0 % 16 == 0
    return pl.pallas_call(
        _mm_nt_kernel,
        out_shape=jax.ShapeDtypeStruct((M, n_out), out_dtype),
        grid=(M // tm, n_out // tn),
        in_specs=[pl.BlockSpec((tm, K), lambda i, j: (i, 0)),
                  pl.BlockSpec((pl.Element(1), pl.Element(tn), pl.Element(K)),
                               lambda i, j: (layer, pl.multiple_of(row0 + j * tn, 16), 0))],
        out_specs=pl.BlockSpec((tm, tn), lambda i, j: (i, j)),
        compiler_params=_params(("parallel", "arbitrary")),
        name=name,
    )(a, wt)


def _w_spec(w, layer, tn, col):
    if w.ndim == 2:
        return pl.BlockSpec((w.shape[0], tn), lambda i, j: (0, col(j)))
    return pl.BlockSpec((None, w.shape[1], tn), lambda i, j: (layer, 0, col(j)))


def _ffn_up_ws_kernel(h_ref, wg_ref, wu_ref, o_ref, wgb_ref, wub_ref):
    @pl.when(pl.program_id(1) == 0)
    def _():
        wgb_ref[...] = wg_ref[...].astype(BF16)
        wub_ref[...] = wu_ref[...].astype(BF16)

    h = h_ref[...]
    gate = jnp.dot(h, wgb_ref[...], preferred_element_type=F32)
    up = jnp.dot(h, wub_ref[...], preferred_element_type=F32)
    o_ref[...] = (_silu(gate) * up).astype(o_ref.dtype)


def ffn_up_ws(h, wg_stack, wu_stack, layer, tm, tn):
    M, K = h.shape
    N = wg_stack.shape[2]
    wspec = pl.BlockSpec((None, K, tn), lambda j, i: (layer, 0, j))
    return pl.pallas_call(
        _ffn_up_ws_kernel,
        out_shape=jax.ShapeDtypeStruct((M, N), BF16),
        grid=(N // tn, M // tm),
        in_specs=[pl.BlockSpec((tm, K), lambda j, i: (i, 0)), wspec, wspec],
        out_specs=pl.BlockSpec((tm, tn), lambda j, i: (i, j)),
        scratch_shapes=[pltpu.VMEM((K, tn), BF16)] * 2,
        compiler_params=_params(("arbitrary", "arbitrary")),
        name="ffn_up",
    )(h, wg_stack, wu_stack)


def _mm_res_kernel(a_ref, w_ref, x_ref, o_ref):
    o_ref[...] = x_ref[...] + jnp.dot(a_ref[...], w_ref[...], preferred_element_type=F32)


def matmul_residual(a, w, x, tm, tn, name, layer=0):
    M, K = a.shape
    N = w.shape[-1]
    return pl.pallas_call(
        _mm_res_kernel,
        out_shape=jax.ShapeDtypeStruct((M, N), F32),
        grid=(M // tm, N // tn),
        in_specs=[pl.BlockSpec((tm, K), lambda i, j: (i, 0)),
                  _w_spec(w, layer, tn, lambda j: j),
                  pl.BlockSpec((tm, tn), lambda i, j: (i, j))],
        out_specs=pl.BlockSpec((tm, tn), lambda i, j: (i, j)),
        compiler_params=_params(("parallel", "arbitrary")),
        name=name,
    )(a, w, x)


def _merge_proj_kernel(ya_ref, yr_ref, ys_ref, pa_ref, pr_ref, ps_ref,
                       ga_ref, gr_ref, gs_ref, o_ref):
    def sig(ref):
        return _sigmoid(ref[...].astype(F32))

    acc = sig(ga_ref) * jnp.dot(ya_ref[...], pa_ref[...], preferred_element_type=F32)
    acc += sig(gr_ref) * jnp.dot(yr_ref[...], pr_ref[...], preferred_element_type=F32)
    acc += sig(gs_ref) * jnp.dot(ys_ref[...], ps_ref[...], preferred_element_type=F32)
    o_ref[...] = acc.astype(o_ref.dtype)


def merge_proj(y_att, y_rwkv, y_ssm, p_att, p_rwkv, p_ssm, gates, tm, tn, layer):
    M = y_att.shape[0]
    N = p_att.shape[-1]
    nb = N // tn
    return pl.pallas_call(
        _merge_proj_kernel,
        out_shape=jax.ShapeDtypeStruct((M, N), BF16),
        grid=(M // tm, nb),
        in_specs=[pl.BlockSpec((tm, y_att.shape[1]), lambda i, j: (i, 0)),
                  pl.BlockSpec((tm, y_rwkv.shape[1]), lambda i, j: (i, 0)),
                  pl.BlockSpec((tm, y_ssm.shape[1]), lambda i, j: (i, 0)),
                  _w_spec(p_att, layer, tn, lambda j: j),
                  _w_spec(p_rwkv, layer, tn, lambda j: j),
                  _w_spec(p_ssm, layer, tn, lambda j: j),
                  pl.BlockSpec((tm, tn), lambda i, j: (i, j)),
                  pl.BlockSpec((tm, tn), lambda i, j: (i, j + nb)),
                  pl.BlockSpec((tm, tn), lambda i, j: (i, j + 2 * nb))],
        out_specs=pl.BlockSpec((tm, tn), lambda i, j: (i, j)),
        compiler_params=_params(("parallel", "arbitrary")),
        name="merge_proj",
    )(y_att, y_rwkv, y_ssm, p_att, p_rwkv, p_ssm, gates, gates, gates)


ATT_WIN = 2048


def _rows(start, size, stride):
    return pl.ds(start, size) if stride == 1 else pl.ds(start, size, stride=stride)


def _attn_kernel(*refs):
    n_g = len(ATT_GROUPS)
    ins = refs[:5 * n_g]
    y_ref = refs[5 * n_g]
    scr = refs[5 * n_g + 1:]
    kext, vext = scr[0:n_g], scr[n_g:2 * n_g]
    o_scr, l_scr = scr[2 * n_g], scr[2 * n_g + 1]
    w = pl.program_id(1)
    blk = ATT_DIM
    qi = lax.broadcasted_iota(jnp.int32, (blk, 2 * blk), 0)
    kj = lax.broadcasted_iota(jnp.int32, (blk, 2 * blk), 1)
    dist = qi + blk - kj
    scale = ATT_DIM ** -0.5

    for g, (window, d) in enumerate(ATT_GROUPS):
        q_ref, k_ref, v_ref, kp_ref, vp_ref = ins[5 * g:5 * g + 5]
        halo = d * blk
        kext[g][0:halo, :] = kp_ref[...]
        kext[g][halo:halo + ATT_WIN, :] = k_ref[...]
        vext[g][0:halo, :] = vp_ref[...]
        vext[g][halo:halo + ATT_WIN, :] = v_ref[...]
        band = (dist >= 0) & (dist <= window // d)
        band_first = band & jnp.logical_or(w > 0, kj >= blk)
        for res in range(d):
            for m in range(ATT_WIN // halo):
                row0 = res + halo * m
                q = q_ref[_rows(row0, blk, d), :].astype(BF16)
                k = kext[g][_rows(row0, 2 * blk, d), :].astype(BF16)
                v = vext[g][_rows(row0, 2 * blk, d), :].astype(BF16)
                s = lax.dot_general(q, k, (((1,), (1,)), ((), ())), preferred_element_type=F32) * scale
                s = jnp.where(band if m > 0 else band_first, s, -jnp.inf)
                mx = jnp.max(s, axis=-1, keepdims=True)
                p = jnp.exp(s - mx)
                den = jnp.sum(p, axis=-1, keepdims=True)
                o = jnp.dot((p / den).astype(BF16), v, preferred_element_type=F32)
                o_scr[g, _rows(row0, blk, d), :] = o
                l_scr[g, _rows(row0, blk, d), :] = jnp.broadcast_to(mx + jnp.log(den), (blk, blk))

    rc = 256
    for c in range(ATT_WIN // rc):
        rows = slice(c * rc, (c + 1) * rc)
        la, lb, lc = l_scr[0, rows, :], l_scr[1, rows, :], l_scr[2, rows, :]
        mx = jnp.maximum(jnp.maximum(la, lb), lc)
        ea, eb, ec = jnp.exp(la - mx), jnp.exp(lb - mx), jnp.exp(lc - mx)
        y = (ea * o_scr[0, rows, :] + eb * o_scr[1, rows, :] + ec * o_scr[2, rows, :]) / (ea + eb + ec)
        y_ref[rows, :] = y.astype(y_ref.dtype)


def attention_mixer(qkv):
    T = qkv.shape[0]
    blk = ATT_DIM
    n_g = len(ATT_GROUPS)
    sect = n_g * ATT_HEADS
    in_specs, scratch_k, scratch_v = [], [], []
    for g, (window, d) in enumerate(ATT_GROUPS):
        per_win = ATT_WIN // (d * blk)
        for s in range(3):
            in_specs.append(pl.BlockSpec((ATT_WIN, blk),
                                         lambda h, w, s=s, g=g: (w, s * sect + g * ATT_HEADS + h)))
        for s in (1, 2):
            in_specs.append(pl.BlockSpec(
                (d * blk, blk),
                lambda h, w, s=s, g=g, per_win=per_win: (jnp.maximum(w * per_win - 1, 0),
                                                         s * sect + g * ATT_HEADS + h)))
        scratch_k.append(pltpu.VMEM((ATT_WIN + d * blk, blk), F32))
        scratch_v.append(pltpu.VMEM((ATT_WIN + d * blk, blk), F32))
    return pl.pallas_call(
        _attn_kernel,
        out_shape=jax.ShapeDtypeStruct((T, ATT_OUT), BF16),
        grid=(ATT_HEADS, T // ATT_WIN),
        in_specs=in_specs,
        out_specs=pl.BlockSpec((ATT_WIN, blk), lambda h, w: (w, h)),
        scratch_shapes=scratch_k + scratch_v + [pltpu.VMEM((n_g, ATT_WIN, blk), F32)] * 2,
        compiler_params=_params(("parallel", "arbitrary")),
        name="attention",
    )(*([qkv] * (5 * n_g)))


def _head_sum_matrix():
    i = lax.broadcasted_iota(jnp.int32, (LANES, LANES), 0) // RWKV_HEAD
    j = lax.broadcasted_iota(jnp.int32, (LANES, LANES), 1) // RWKV_HEAD
    return jnp.where(i == j, 1.0, 0.0).astype(BF16)


def _head_sums(x, bd):
    cols = []
    for c in range(x.shape[1] // LANES):
        cols.append(_dot_exact_rhs(x[:, c * LANES:(c + 1) * LANES], bd))
    return jnp.concatenate(cols, axis=1)


def _rwkv_prep_kernel(u_ref, mu_ref, w0_ref, w2_ref, a0_ref, a2_ref, g2_ref, kk_ref, ka_ref, rk_ref,
                      r_out, lw_out, k_out, v_out, kkn_out, a_out, g_out, bonus_out, carry_ref):
    i = pl.program_id(0)
    tb = u_ref.shape[0]
    W = RWKV_WIDTH
    CW = 2 * LANES

    @pl.when(i == 0)
    def _():
        carry_ref[...] = jnp.zeros_like(carry_ref)

    def mixed(lo, width):
        sl = slice(lo, lo + width)
        u = u_ref[:, sl]
        row = lax.broadcasted_iota(jnp.int32, u.shape, 0)
        u_prev = jnp.where(row == 0, jnp.broadcast_to(carry_ref[0:1, sl], u.shape), pltpu.roll(u, 1, 0))
        return u + (u_prev - u) * mu_ref[:, sl]

    o = 3 * W
    th = jnp.tanh(mixed(o, RWKV_LORA_W))
    xa = mixed(o + RWKV_LORA_W, RWKV_LORA_A)
    sg = _sigmoid(mixed(o + RWKV_LORA_W + RWKV_LORA_A, RWKV_G_PAD))
    bd = _head_sum_matrix()
    for cc in range(W // CW):
        sl = slice(cc * CW, (cc + 1) * CW)
        r = mixed(cc * CW, CW)
        k = mixed(W + cc * CW, CW)
        v = mixed(2 * W + cc * CW, CW)
        w_log = -_softplus(-(w0_ref[:, sl] + _dot_hi(th, w2_ref[:, sl]))) - 0.5
        lw_out[:, sl] = -jnp.exp(w_log)
        a = _sigmoid(a0_ref[:, sl] + _dot_hi(xa, a2_ref[:, sl]))
        g_out[:, sl] = _dot_hi(sg, g2_ref[:, sl]).astype(g_out.dtype)
        kk = k * kk_ref[:, sl]
        kk = kk / jnp.maximum(jnp.sqrt(_head_sums(kk * kk, bd)), 1e-12)
        k2 = k * (1.0 + (a - 1.0) * ka_ref[:, sl])
        bonus_out[:, sl] = (_head_sums(r * k2 * rk_ref[:, sl], bd) * v).astype(bonus_out.dtype)
        r_out[:, sl] = r.astype(r_out.dtype)
        k_out[:, sl] = k2.astype(k_out.dtype)
        v_out[:, sl] = v.astype(v_out.dtype)
        kkn_out[:, sl] = kk.astype(kkn_out.dtype)
        a_out[:, sl] = a.astype(a_out.dtype)
    carry_ref[0:1, :] = u_ref[tb - 1:tb, :]


def rwkv_prep(u, mu, w0, w2, a0, a2, g2, k_k, k_a, r_k, tb=128):
    T = u.shape[0]
    W = RWKV_WIDTH
    row = lambda n: pl.BlockSpec((1, n), lambda i: (0, 0))
    full = lambda a: pl.BlockSpec(a.shape, lambda i: (0, 0))
    out_spec = pl.BlockSpec((tb, W), lambda i: (i, 0))
    return pl.pallas_call(
        _rwkv_prep_kernel,
        out_shape=[jax.ShapeDtypeStruct((T, W), F32 if i == 1 else BF16) for i in range(8)],
        grid=(T // tb,),
        in_specs=[pl.BlockSpec((tb, RWKV_PAD), lambda i: (i, 0)), row(RWKV_PAD), row(W), full(w2),
                  row(W), full(a2), full(g2), row(W), row(W), row(W)],
        out_specs=[out_spec] * 8,
        scratch_shapes=[pltpu.VMEM((8, RWKV_PAD), F32)],
        compiler_params=_params(("arbitrary",)),
        name="rwkv_prep",
    )(u, mu.reshape(1, -1), w0.reshape(1, W), w2, a0.reshape(1, W), a2, g2,
      k_k.reshape(1, W), k_a.reshape(1, W), r_k.reshape(1, W))


def _rwkv_chunk(ins, states, consts):
    tril_l, m0, strict, incl, eye = consts
    P = range(len(ins))
    L = ins[0][0].shape[0]
    n2 = 2 * L

    def stack(x):
        return jnp.concatenate([jnp.where(m0, x, 0.0), jnp.where(m0, 0.0, x)], axis=0)

    c = [_dot_exact_lhs(tril_l, ins[p][1]) for p in P]
    c_last = [c[p][L - 1:L, :] for p in P]
    lhs, rhs, r_s, a_s, v_s, lhs2 = [], [], [], [], [], []
    for p in P:
        r, lw, k, v, kk, a = ins[p]
        n_in = jnp.exp(-c[p])
        to_end = jnp.exp(c_last[p] - c[p])
        a_t = -kk * jnp.exp(c[p] - lw)
        b_raw = kk * a
        b_t = b_raw * n_in
        k_t = k * n_in
        a_s.append(stack(a_t))
        r_s.append(stack(r * jnp.exp(c[p])))
        v_s.append(stack(v))
        lhs.append(jnp.concatenate([a_s[p], r_s[p]], axis=0))
        rhs.append(jnp.concatenate([b_t, b_t, k_t, k_t], axis=0))
        lhs2.append(jnp.concatenate([stack(b_raw * to_end), stack(k * to_end)], axis=0))
    big = [_bdot_nt(lhs[p], rhs[p]) for p in P]
    a_ab = [jnp.where(strict, big[p][0:n2, 0:n2], 0.0) for p in P]
    a_ak = [jnp.where(strict, big[p][0:n2, n2:2 * n2], 0.0) for p in P]
    m_rb = [jnp.where(incl, big[p][n2:2 * n2, 0:n2], 0.0) for p in P]
    m_rk = [jnp.where(incl, big[p][n2:2 * n2, n2:2 * n2], 0.0) for p in P]

    t_inv = [eye + a_ab[p] for p in P]
    pw = [_bdot(a_ab[p], a_ab[p]) for p in P]
    av = [_bdot(a_ak[p], v_s[p]) for p in P]
    steps = (L - 1).bit_length() - 1
    for s in range(steps):
        if s < steps - 1:
            res = [_bdot(jnp.concatenate([t_inv[p], pw[p]], axis=0), pw[p]) for p in P]
            t_inv = [t_inv[p] + res[p][0:n2] for p in P]
            pw = [res[p][n2:2 * n2] for p in P]
        else:
            res = [_bdot(t_inv[p], pw[p]) for p in P]
            t_inv = [t_inv[p] + res[p] for p in P]

    sol = [_bdot(t_inv[p], jnp.concatenate([a_s[p], av[p]], axis=1)) for p in P]
    qy = [_bdot(m_rb[p], sol[p]) for p in P]
    yv = [_bdot(m_rk[p], v_s[p]) for p in P]
    gh = [_bdot_tn(lhs2[p], jnp.concatenate(
        [sol[p], jnp.concatenate([jnp.zeros_like(v_s[p]), v_s[p]], axis=1)], axis=0)) for p in P]
    g_mm = [gh[p][:, 0:LANES] for p in P]
    h_mm = [gh[p][:, LANES:2 * LANES] for p in P]
    outs = []
    for p in P:
        q_s = r_s[p] + qy[p][:, 0:LANES]
        y_s = qy[p][:, LANES:2 * LANES] + yv[p]
        q_h = q_s[0:L] + q_s[L:n2]
        y_in = y_s[0:L] + y_s[L:n2]
        g_mat = g_mm[p] + jnp.where(
            eye > 0, jnp.broadcast_to(jnp.exp(c_last[p]), (LANES, LANES)), 0.0)
        y = _bdot(q_h, states[p]) + y_in
        s_new = _bdot(g_mat, states[p]) + h_mm[p]
        outs.append((y, s_new))
    return outs


def _rwkv_scan_kernel(r_ref, lw_ref, k_ref, v_ref, kk_ref, a_ref, g_ref, bonus_ref, lnw_ref, lnb_ref,
                      y_ref, s_ref, yp_ref, *, pairs):
    c = pl.program_id(1)
    L = r_ref.shape[0]
    n2 = 2 * L

    @pl.when(c == 0)
    def _():
        s_ref[...] = jnp.zeros_like(s_ref)
        yp_ref[...] = jnp.zeros_like(yp_ref)

    ti = lax.broadcasted_iota(jnp.int32, (L, L), 0)
    tj = lax.broadcasted_iota(jnp.int32, (L, L), 1)
    tril_l = jnp.where(ti >= tj, 1.0, 0.0).astype(BF16)
    m0 = lax.broadcasted_iota(jnp.int32, (L, LANES), 1) < RWKV_HEAD
    si = lax.broadcasted_iota(jnp.int32, (n2, n2), 0)
    sj = lax.broadcasted_iota(jnp.int32, (n2, n2), 1)
    same = (si // L) == (sj // L)
    strict = same & ((sj % L) < (si % L))
    incl = same & ((sj % L) <= (si % L))
    eye = jnp.where(si == sj, 1.0, 0.0).astype(F32)
    consts = (tril_l, m0, strict, incl, eye)

    def half_sums(x):
        s0 = jnp.sum(jnp.where(m0, x, 0.0), axis=-1, keepdims=True)
        s1 = jnp.sum(jnp.where(m0, 0.0, x), axis=-1, keepdims=True)
        return jnp.where(m0, s0, s1)

    sls = [slice(p * LANES, (p + 1) * LANES) for p in range(pairs)]
    ins = [tuple(ref[:, sl].astype(F32) for ref in (r_ref, lw_ref, k_ref, v_ref, kk_ref, a_ref))
           for sl in sls]
    outs = _rwkv_chunk(ins, [s_ref[p] for p in range(pairs)], consts)
    cur = c % 2
    for p in range(pairs):
        yp_ref[cur, p] = outs[p][0]
        s_ref[p] = outs[p][1]

    inv_n = 1.0 / RWKV_HEAD
    for p in range(pairs):
        sl = sls[p]
        y = yp_ref[1 - cur, p]
        yc = y - half_sums(y) * inv_n
        var = half_sums(yc * yc) * inv_n
        out = yc * lax.rsqrt(var + RWKV_LN_EPS) * lnw_ref[:, sl] + lnb_ref[:, sl] + bonus_ref[:, sl].astype(F32)
        y_ref[:, sl] = (out * g_ref[:, sl].astype(F32)).astype(y_ref.dtype)


def rwkv_scan(r, lw, k, v, kk, a, g, bonus, ln_w, ln_b, pairs=8):
    T, W = r.shape
    L = RWKV_CHUNK
    wb = pairs * LANES
    nc = T // L
    spec = pl.BlockSpec((L, wb), lambda p, c: (jnp.minimum(c, nc - 1), p))
    late = pl.BlockSpec((L, wb), lambda p, c: (jnp.maximum(c - 1, 0), p))
    row = pl.BlockSpec((1, wb), lambda p, c: (0, p))
    return pl.pallas_call(
        functools.partial(_rwkv_scan_kernel, pairs=pairs),
        out_shape=jax.ShapeDtypeStruct((T, W), BF16),
        grid=(W // wb, nc + 1),
        in_specs=[spec] * 6 + [late, late, row, row],
        out_specs=late,
        scratch_shapes=[pltpu.VMEM((pairs, LANES, LANES), F32), pltpu.VMEM((2, pairs, L, LANES), F32)],
        compiler_params=_params(("parallel", "arbitrary")),
        name="rwkv_scan",
    )(r, lw, k, v, kk, a, g, bonus, ln_w.reshape(1, W), ln_b.reshape(1, W))


def rwkv_mixer(u, mu, w0, w2, a0, a2, g2, k_k, k_a, r_k, ln_w, ln_b):
    r, lw, k, v, kk, a, g, bonus = rwkv_prep(u, mu, w0, w2, a0, a2, g2, k_k, k_a, r_k)
    return rwkv_scan(r, lw, k, v, kk, a, g, bonus, ln_w, ln_b)


def _ssd_kernel(p_ref, cw_ref, cb_ref, dtb_ref, alog_ref, dsk_ref, ng_ref, o_ref,
                ext_ref, xbc_ref, st_ref, y_scr):
    c = pl.program_id(0)
    L = SSM_CHUNK
    X0 = SSM_INNER
    HP = 2 * 64

    @pl.when(c == 0)
    def _():
        ext_ref[0:8, :] = jnp.zeros((8, SSM_CONV_DIM), F32)
        st_ref[...] = jnp.zeros_like(st_ref)

    ext_ref[8:8 + L, :] = p_ref[:, X0:X0 + SSM_CONV_DIM]
    CW = 4 * LANES
    for cc in range(SSM_CONV_DIM // CW):
        sl = slice(cc * CW, (cc + 1) * CW)
        conv = jnp.broadcast_to(cb_ref[:, sl], (L, CW))
        for kk in range(SSM_CONV):
            conv = conv + cw_ref[kk:kk + 1, sl] * ext_ref[pl.ds(8 - (SSM_CONV - 1) + kk, L), sl]
        xbc_ref[:, sl] = _silu(conv)
    ext_ref[0:8, :] = p_ref[L - 8:L, X0:X0 + SSM_CONV_DIM]

    dt = _softplus(p_ref[:, X0 + SSM_CONV_DIM:X0 + SSM_CONV_DIM + LANES] + dtb_ref[...])
    d_a = dt * (-jnp.exp(alog_ref[...]))
    ti = lax.broadcasted_iota(jnp.int32, (L, L), 0)
    tj = lax.broadcasted_iota(jnp.int32, (L, L), 1)
    causal = ti >= tj
    a_cum = _dot_exact_lhs(jnp.where(causal, 1.0, 0.0).astype(BF16), d_a)
    a_cum_t = a_cum.T
    dt_t = dt.T
    lane = lax.broadcasted_iota(jnp.int32, (L, HP), 1)
    first = lane < 64

    for g in range(SSM_GROUPS):
        b_m = xbc_ref[:,SSM_INNER + g * SSM_STATE:SSM_INNER + (g + 1) * SSM_STATE]
        c_m = xbc_ref[:,SSM_INNER + SSM_GROUPS * SSM_STATE + g * SSM_STATE:
                  SSM_INNER + SSM_GROUPS * SSM_STATE + (g + 1) * SSM_STATE]
        cb = _bdot_nt(c_m, b_m)
        b_t = b_m.T
        for j in range(3):
            pair = g * 3 + j
            xs = xbc_ref[:,pair * HP:(pair + 1) * HP]
            yd, st, ea, cd = [], [], [], []
            for hh in (2 * pair, 2 * pair + 1):
                col = a_cum[:, hh:hh + 1]
                row = a_cum_t[hh:hh + 1, :]
                dtr = dt_t[hh:hh + 1, :]
                last = a_cum[L - 1:L, hh:hh + 1]
                lmat = jnp.exp(jnp.where(causal, col - row, -jnp.inf))
                yd.append(_bdot(cb * lmat * dtr, xs))
                st.append(_bdot(b_t * (jnp.exp(last - row) * dtr), xs))
                ea.append(jnp.exp(col))
                cd.append(jnp.exp(last))
            s_in = st_ref[pair]
            y = jnp.where(first, yd[0], yd[1])
            y = y + _bdot(c_m, s_in) * jnp.where(first, ea[0], ea[1])
            st_ref[pair] = s_in * jnp.where(first, cd[0], cd[1]) + jnp.where(first, st[0], st[1])
            y_scr[:, pair * HP:(pair + 1) * HP] = y + xs * dsk_ref[:, pair * HP:(pair + 1) * HP]

    gw = SSM_INNER // SSM_GROUPS
    for g in range(SSM_GROUPS):
        sl = slice(g * gw, (g + 1) * gw)
        yg = y_scr[:, sl] * _silu(p_ref[:, sl])
        ms = jnp.mean(yg * yg, axis=-1, keepdims=True)
        o_ref[:, sl] = (yg * lax.rsqrt(ms + SSM_NORM_EPS) * ng_ref[:, sl]).astype(o_ref.dtype)


def mamba2_mixer(u, conv_w, conv_b, dt_bias, a_log, d_skip, norm_g):
    T = u.shape[0]
    L = SSM_CHUNK
    padh = lambda x: jnp.pad(x, (0, LANES - SSM_HEADS)).reshape(1, LANES)
    full = lambda a: pl.BlockSpec(a.shape, lambda i: (0,) * a.ndim)
    args = (u, conv_w, conv_b.reshape(1, -1), padh(dt_bias), padh(a_log),
            jnp.repeat(d_skip, 64).reshape(1, SSM_INNER), norm_g.reshape(1, SSM_INNER))
    return pl.pallas_call(
        _ssd_kernel,
        out_shape=jax.ShapeDtypeStruct((T, SSM_INNER), BF16),
        grid=(T // L,),
        in_specs=[pl.BlockSpec((L, SSM_PAD), lambda i: (i, 0))] + [full(a) for a in args[1:]],
        out_specs=pl.BlockSpec((L, SSM_INNER), lambda i: (i, 0)),
        scratch_shapes=[pltpu.VMEM((L + 8, SSM_CONV_DIM), F32),
                        pltpu.VMEM((L, SSM_CONV_DIM), F32),
                        pltpu.VMEM((SSM_HEADS // 2, SSM_STATE, LANES), F32),
                        pltpu.VMEM((L, SSM_INNER), F32)],
        compiler_params=_params(("arbitrary",)),
        name="ssd",
    )(*args)


def kernel(x, norm_mix_g, w_in, rwkv_mu, rwkv_w0, rwkv_w2, rwkv_a0, rwkv_a2, rwkv_g2, rwkv_k_k, rwkv_k_a, rwkv_r_k, rwkv_ln_w, rwkv_ln_b, ssm_conv_w, ssm_conv_b, ssm_dt_bias, ssm_a_log, ssm_d, ssm_norm_g, p_attn, p_rwkv, p_ssm, w_out, norm_ffn_g, w_ffn_gate, w_ffn_up, w_ffn_down, norm_final_g):
    B, T, D = x.shape
    assert B == 1 and D == D_MODEL
    xt = x.reshape(T, D)
    o_r = ATT_COLS
    o_s = ATT_COLS + RWKV_COLS
    o_g = ATT_COLS + RWKV_COLS + SSM_COLS
    TM, TN = 1024, 512
    p_attn_b, p_rwkv_b, p_ssm_b = p_attn.astype(BF16), p_rwkv.astype(BF16), p_ssm.astype(BF16)
    w_out_b, w_down_b = w_out.astype(BF16), w_ffn_down.astype(BF16)
    wt_in = jnp.swapaxes(w_in, 1, 2).astype(BF16)
    for l in range(DEPTH):
        h = rms_norm(xt, norm_mix_g[l], BF16)
        qkv = matmul_nt(h, wt_in, l, 0, ATT_COLS, F32, TM, 2 * TN, "proj_att")
        u_rwkv = matmul_nt(h, wt_in, l, o_r, RWKV_PAD, F32, TM, 2 * TN, "proj_rwkv")
        u_ssm = matmul_nt(h, wt_in, l, o_s, SSM_PAD, F32, TM, TN, "proj_ssm")
        gates = matmul_nt(h, wt_in, l, o_g, GATE_COLS, BF16, TM, 2 * TN, "proj_gate")

        y_att = attention_mixer(qkv)
        g2 = jnp.pad(rwkv_g2[l], ((0, RWKV_G_PAD - RWKV_LORA_G), (0, 0)))
        mu = jnp.pad(rwkv_mu[l], (0, RWKV_PAD - RWKV_COLS))
        y_rwkv = rwkv_mixer(u_rwkv, mu, rwkv_w0[l], rwkv_w2[l], rwkv_a0[l], rwkv_a2[l], g2,
                            rwkv_k_k[l], rwkv_k_a[l], rwkv_r_k[l].reshape(-1), rwkv_ln_w[l], rwkv_ln_b[l])
        y_ssm = mamba2_mixer(u_ssm, ssm_conv_w[l], ssm_conv_b[l], ssm_dt_bias[l], ssm_a_log[l],
                             ssm_d[l], ssm_norm_g[l])

        merged = merge_proj(y_att, y_rwkv, y_ssm, p_attn_b, p_rwkv_b, p_ssm_b, gates, TM // 2, TN, l)
        xt = matmul_residual(merged, w_out_b, xt, TM, TN, "w_out", l)

        h2 = rms_norm(xt, norm_ffn_g[l], BF16)
        act = ffn_up_ws(h2, w_ffn_gate, w_ffn_up, l, TM, 256)
        xt = matmul_residual(act, w_down_b, xt, TM // 2, TN, "ffn_down", l)
    out = rms_norm(xt, norm_final_g, F32)
    return out.reshape(B, T, D)
```

```python
import functools

import jax
import jax.numpy as jnp
from jax import lax
from jax.experimental import pallas as pl
from jax.experimental.pallas import tpu as pltpu

F32 = jnp.float32
BF16 = jnp.bfloat16

D_MODEL = 4096
DEPTH = 2
NORM_EPS = 1e-6
ATT_GROUPS = ((128, 1), (512, 4), (2048, 16))
ATT_HEADS = 8
ATT_DIM = 128
ATT_WIDTH = 3072
ATT_OUT = 1024
ATT_COLS = 9216
RWKV_WIDTH = 3072
RWKV_HEAD = 64
RWKV_LORA_W = 128
RWKV_LORA_A = 128
RWKV_LORA_G = 480
RWKV_COLS = 9952
RWKV_LN_EPS = 64e-5
SSM_INNER = 3072
SSM_HEADS = 48
SSM_GROUPS = 8
SSM_STATE = 128
SSM_CONV = 4
SSM_CHUNK = 128
SSM_CONV_DIM = 5120
SSM_COLS = 8240
SSM_NORM_EPS = 1e-5
FFN_HIDDEN = 11008
GATE_COLS = 3 * D_MODEL

RWKV_PAD = 10240
RWKV_G_PAD = 512
SSM_PAD = 8704

LANES = 128
RWKV_CHUNK = 64
VMEM_LIMIT = 56 * 1024 * 1024
FFN_UP_VMEM_LIMIT = 61 * 1024 * 1024


def _params(sem, vmem_limit=VMEM_LIMIT):
    return pltpu.CompilerParams(dimension_semantics=sem, vmem_limit_bytes=vmem_limit)


def _bdot(a, b):
    return jnp.dot(a.astype(BF16), b.astype(BF16), preferred_element_type=F32)


def _bdot_nt(a, b):
    return lax.dot_general(a.astype(BF16), b.astype(BF16), (((1,), (1,)), ((), ())),
                           preferred_element_type=F32)


def _bdot_tn(a, b):
    return lax.dot_general(a.astype(BF16), b.astype(BF16), (((0,), (0,)), ((), ())),
                           preferred_element_type=F32)


def _split2(x):
    hi = x.astype(BF16)
    lo = (x - hi.astype(F32)).astype(BF16)
    return hi, lo


def _split3(x):
    hi = x.astype(BF16)
    r1 = x - hi.astype(F32)
    mid = r1.astype(BF16)
    lo = (r1 - mid.astype(F32)).astype(BF16)
    return hi, mid, lo


def _dot_exact_lhs(m_bf16, x):
    hi, mid, lo = _split3(x)
    d = functools.partial(jnp.dot, preferred_element_type=F32)
    return d(m_bf16, hi) + d(m_bf16, mid) + d(m_bf16, lo)


def _sigmoid(x):
    return 1.0 / (1.0 + jnp.exp(-x))


def _silu(x):
    return x * _sigmoid(x)


def _softplus(x):
    return jnp.maximum(x, 0.0) + jnp.log(1.0 + jnp.exp(-jnp.abs(x)))


def _rms_kernel(x_ref, g_ref, o_ref):
    x = x_ref[...]
    ms = jnp.mean(x * x, axis=-1, keepdims=True)
    o_ref[...] = (x * lax.rsqrt(ms + NORM_EPS) * g_ref[...]).astype(o_ref.dtype)


def rms_norm(x, g, out_dtype, tr=256):
    T, D = x.shape
    return pl.pallas_call(
        _rms_kernel,
        out_shape=jax.ShapeDtypeStruct((T, D), out_dtype),
        grid=(T // tr,),
        in_specs=[pl.BlockSpec((tr, D), lambda i: (i, 0)),
                  pl.BlockSpec((1, D), lambda i: (0, 0))],
        out_specs=pl.BlockSpec((tr, D), lambda i: (i, 0)),
        compiler_params=_params(("parallel",)),
        name="rms_norm",
    )(x, g.reshape(1, D))


def _mm_nt_kernel(a_ref, w_ref, o_ref):
    o_ref[...] = lax.dot_general(a_ref[...], w_ref[0], (((1,), (1,)), ((), ())),
                                 preferred_element_type=F32).astype(o_ref.dtype)


def matmul_nt(a, wt, layer, row0, n_out, out_dtype, tm, tn, name):
    M, K = a.shape
    assert row0 % 16 == 0
    return pl.pallas_call(
        _mm_nt_kernel,
        out_shape=jax.ShapeDtypeStruct((M, n_out), out_dtype),
        grid=(M // tm, n_out // tn),
        in_specs=[pl.BlockSpec((tm, K), lambda i, j: (i, 0)),
                  pl.BlockSpec((pl.Element(1), pl.Element(tn), pl.Element(K)),
                               lambda i, j: (layer, pl.multiple_of(row0 + j * tn, 16), 0))],
        out_specs=pl.BlockSpec((tm, tn), lambda i, j: (i, j)),
        compiler_params=_params(("parallel", "arbitrary")),
        name=name,
    )(a, wt)


def _w_spec(w, layer, tn, col):
    if w.ndim == 2:
        return pl.BlockSpec((w.shape[0], tn), lambda i, j: (0, col(j)))
    return pl.BlockSpec((None, w.shape[1], tn), lambda i, j: (layer, 0, col(j)))


def _ffn_up_ws_kernel(h_ref, wg_ref, wu_ref, o_ref, wgb_ref, wub_ref):
    @pl.when(pl.program_id(1) == 0)
    def _():
        wgb_ref[...] = wg_ref[...].astype(BF16)
        wub_ref[...] = wu_ref[...].astype(BF16)

    h = h_ref[...]
    gate = jnp.dot(h, wgb_ref[...], preferred_element_type=F32)
    up = jnp.dot(h, wub_ref[...], preferred_element_type=F32)
    o_ref[...] = (_silu(gate) * up).astype(o_ref.dtype)


def ffn_up_ws(h, wg_stack, wu_stack, layer, tm, tn):
    M, K = h.shape
    N = wg_stack.shape[2]
    wspec = pl.BlockSpec((None, K, tn), lambda j, i: (layer, 0, j))
    return pl.pallas_call(
        _ffn_up_ws_kernel,
        out_shape=jax.ShapeDtypeStruct((M, N), BF16),
        grid=(N // tn, M // tm),
        in_specs=[pl.BlockSpec((tm, K), lambda j, i: (i, 0)), wspec, wspec],
        out_specs=pl.BlockSpec((tm, tn), lambda j, i: (i, j)),
        scratch_shapes=[pltpu.VMEM((K, tn), BF16)] * 2,
        compiler_params=_params(("arbitrary", "arbitrary"), FFN_UP_VMEM_LIMIT),
        name="ffn_up",
    )(h, wg_stack, wu_stack)


def _mm_res_kernel(a_ref, w_ref, x_ref, o_ref):
    o_ref[...] = x_ref[...] + jnp.dot(a_ref[...], w_ref[...], preferred_element_type=F32)


def matmul_residual(a, w, x, tm, tn, name, layer=0):
    M, K = a.shape
    N = w.shape[-1]
    return pl.pallas_call(
        _mm_res_kernel,
        out_shape=jax.ShapeDtypeStruct((M, N), F32),
        grid=(M // tm, N // tn),
        in_specs=[pl.BlockSpec((tm, K), lambda i, j: (i, 0)),
                  _w_spec(w, layer, tn, lambda j: j),
                  pl.BlockSpec((tm, tn), lambda i, j: (i, j))],
        out_specs=pl.BlockSpec((tm, tn), lambda i, j: (i, j)),
        compiler_params=_params(("parallel", "arbitrary")),
        name=name,
    )(a, w, x)


def _merge_proj_kernel(ya_ref, yr_ref, ys_ref, pa_ref, pr_ref, ps_ref,
                       ga_ref, gr_ref, gs_ref, o_ref):
    def sig(ref):
        return _sigmoid(ref[...].astype(F32))

    acc = sig(ga_ref) * jnp.dot(ya_ref[...], pa_ref[...], preferred_element_type=F32)
    acc += sig(gr_ref) * jnp.dot(yr_ref[...], pr_ref[...], preferred_element_type=F32)
    acc += sig(gs_ref) * jnp.dot(ys_ref[...], ps_ref[...], preferred_element_type=F32)
    o_ref[...] = acc.astype(o_ref.dtype)


def merge_proj(y_att, y_rwkv, y_ssm, p_att, p_rwkv, p_ssm, gates, tm, tn, layer):
    M = y_att.shape[0]
    N = p_att.shape[-1]
    nb = N // tn
    return pl.pallas_call(
        _merge_proj_kernel,
        out_shape=jax.ShapeDtypeStruct((M, N), BF16),
        grid=(M // tm, nb),
        in_specs=[pl.BlockSpec((tm, y_att.shape[1]), lambda i, j: (i, 0)),
                  pl.BlockSpec((tm, y_rwkv.shape[1]), lambda i, j: (i, 0)),
                  pl.BlockSpec((tm, y_ssm.shape[1]), lambda i, j: (i, 0)),
                  _w_spec(p_att, layer, tn, lambda j: j),
                  _w_spec(p_rwkv, layer, tn, lambda j: j),
                  _w_spec(p_ssm, layer, tn, lambda j: j),
                  pl.BlockSpec((tm, tn), lambda i, j: (i, j)),
                  pl.BlockSpec((tm, tn), lambda i, j: (i, j + nb)),
                  pl.BlockSpec((tm, tn), lambda i, j: (i, j + 2 * nb))],
        out_specs=pl.BlockSpec((tm, tn), lambda i, j: (i, j)),
        compiler_params=_params(("parallel", "arbitrary")),
        name="merge_proj",
    )(y_att, y_rwkv, y_ssm, p_att, p_rwkv, p_ssm, gates, gates, gates)


ATT_WIN = 2048


def _rows(start, size, stride):
    return pl.ds(start, size) if stride == 1 else pl.ds(start, size, stride=stride)


def _attn_kernel(*refs):
    n_g = len(ATT_GROUPS)
    ins = refs[:5 * n_g]
    y_ref = refs[5 * n_g]
    scr = refs[5 * n_g + 1:]
    kext, vext = scr[0:n_g], scr[n_g:2 * n_g]
    o_scr, l_scr = scr[2 * n_g], scr[2 * n_g + 1]
    w = pl.program_id(1)
    blk = ATT_DIM
    qi = lax.broadcasted_iota(jnp.int32, (blk, 2 * blk), 0)
    kj = lax.broadcasted_iota(jnp.int32, (blk, 2 * blk), 1)
    dist = qi + blk - kj
    scale = ATT_DIM ** -0.5

    for g, (window, d) in enumerate(ATT_GROUPS):
        q_ref, k_ref, v_ref, kp_ref, vp_ref = ins[5 * g:5 * g + 5]
        halo = d * blk
        kext[g][0:halo, :] = kp_ref[...]
        kext[g][halo:halo + ATT_WIN, :] = k_ref[...]
        vext[g][0:halo, :] = vp_ref[...]
        vext[g][halo:halo + ATT_WIN, :] = v_ref[...]
        band = (dist >= 0) & (dist <= window // d)
        band_first = band & jnp.logical_or(w > 0, kj >= blk)
        for res in range(d):
            for m in range(ATT_WIN // halo):
                row0 = res + halo * m
                q = q_ref[_rows(row0, blk, d), :].astype(BF16)
                k = kext[g][_rows(row0, 2 * blk, d), :].astype(BF16)
                v = vext[g][_rows(row0, 2 * blk, d), :].astype(BF16)
                s = lax.dot_general(q, k, (((1,), (1,)), ((), ())), preferred_element_type=F32) * scale
                s = jnp.where(band if m > 0 else band_first, s, -jnp.inf)
                mx = jnp.max(s, axis=-1, keepdims=True)
                p = jnp.exp(s - mx)
                den = jnp.sum(p, axis=-1, keepdims=True)
                o = jnp.dot((p / den).astype(BF16), v, preferred_element_type=F32)
                o_scr[g, _rows(row0, blk, d), :] = o
                l_scr[g, _rows(row0, blk, d), :] = jnp.broadcast_to(mx + jnp.log(den), (blk, blk))

    rc = 256
    for c in range(ATT_WIN // rc):
        rows = slice(c * rc, (c + 1) * rc)
        la, lb, lc = l_scr[0, rows, :], l_scr[1, rows, :], l_scr[2, rows, :]
        mx = jnp.maximum(jnp.maximum(la, lb), lc)
        ea, eb, ec = jnp.exp(la - mx), jnp.exp(lb - mx), jnp.exp(lc - mx)
        y = (ea * o_scr[0, rows, :] + eb * o_scr[1, rows, :] + ec * o_scr[2, rows, :]) / (ea + eb + ec)
        y_ref[rows, :] = y.astype(y_ref.dtype)


def attention_mixer(qkv):
    T = qkv.shape[0]
    blk = ATT_DIM
    n_g = len(ATT_GROUPS)
    sect = n_g * ATT_HEADS
    in_specs, scratch_k, scratch_v = [], [], []
    for g, (window, d) in enumerate(ATT_GROUPS):
        per_win = ATT_WIN // (d * blk)
        for s in range(3):
            in_specs.append(pl.BlockSpec((ATT_WIN, blk),
                                         lambda h, w, s=s, g=g: (w, s * sect + g * ATT_HEADS + h)))
        for s in (1, 2):
            in_specs.append(pl.BlockSpec(
                (d * blk, blk),
                lambda h, w, s=s, g=g, per_win=per_win: (jnp.maximum(w * per_win - 1, 0),
                                                         s * sect + g * ATT_HEADS + h)))
        scratch_k.append(pltpu.VMEM((ATT_WIN + d * blk, blk), F32))
        scratch_v.append(pltpu.VMEM((ATT_WIN + d * blk, blk), F32))
    return pl.pallas_call(
        _attn_kernel,
        out_shape=jax.ShapeDtypeStruct((T, ATT_OUT), BF16),
        grid=(ATT_HEADS, T // ATT_WIN),
        in_specs=in_specs,
        out_specs=pl.BlockSpec((ATT_WIN, blk), lambda h, w: (w, h)),
        scratch_shapes=scratch_k + scratch_v + [pltpu.VMEM((n_g, ATT_WIN, blk), F32)] * 2,
        compiler_params=_params(("parallel", "arbitrary")),
        name="attention",
    )(*([qkv] * (5 * n_g)))


def _rwkv_prep_kernel(u_ref, mu_ref, w0_ref, w2_ref, a0_ref, a2_ref, g2_ref, kk_ref, ka_ref, rk_ref,
                      r_out, lw_out, k_out, v_out, kkn_out, a_out, g_out, bonus_out, carry_ref):
    i = pl.program_id(0)
    tb = u_ref.shape[0]
    W = RWKV_WIDTH
    CW = 2 * LANES

    @pl.when(i == 0)
    def _():
        carry_ref[...] = jnp.zeros_like(carry_ref)

    def mixed(lo, width):
        sl = slice(lo, lo + width)
        u = u_ref[:, sl]
        row = lax.broadcasted_iota(jnp.int32, u.shape, 0)
        u_prev = jnp.where(row == 0, jnp.broadcast_to(carry_ref[0:1, sl], u.shape), pltpu.roll(u, 1, 0))
        return u + (u_prev - u) * mu_ref[:, sl]

    def dot3(a_split, b_ref, sl):
        ah, al = a_split
        return jnp.dot(jnp.concatenate([ah, ah, al], axis=1), b_ref[:, sl], preferred_element_type=F32)

    m0 = lax.broadcasted_iota(jnp.int32, (tb, LANES), 1) < RWKV_HEAD

    def head_sums(x):
        cols = []
        for c in range(CW // LANES):
            xb = x[:, c * LANES:(c + 1) * LANES]
            s0 = jnp.sum(jnp.where(m0, xb, 0.0), axis=-1, keepdims=True)
            s1 = jnp.sum(jnp.where(m0, 0.0, xb), axis=-1, keepdims=True)
            cols.append(jnp.where(m0, s0, s1))
        return jnp.concatenate(cols, axis=1)

    o = 3 * W
    th = _split2(jnp.tanh(mixed(o, RWKV_LORA_W)))
    xa = _split2(mixed(o + RWKV_LORA_W, RWKV_LORA_A))
    sg = _split2(_sigmoid(mixed(o + RWKV_LORA_W + RWKV_LORA_A, RWKV_G_PAD)))
    for cc in range(W // CW):
        sl = slice(cc * CW, (cc + 1) * CW)
        r = mixed(cc * CW, CW)
        k = mixed(W + cc * CW, CW)
        v = mixed(2 * W + cc * CW, CW)
        w_log = -_softplus(-(w0_ref[:, sl] + dot3(th, w2_ref, sl))) - 0.5
        lw_out[:, sl] = -jnp.exp(w_log)
        a = _sigmoid(a0_ref[:, sl] + dot3(xa, a2_ref, sl))
        g_out[:, sl] = dot3(sg, g2_ref, sl).astype(g_out.dtype)
        kk = k * kk_ref[:, sl]
        kk = kk / jnp.maximum(jnp.sqrt(head_sums(kk * kk)), 1e-12)
        k2 = k * (1.0 + (a - 1.0) * ka_ref[:, sl])
        bonus_out[:, sl] = (head_sums(r * k2 * rk_ref[:, sl]) * v).astype(bonus_out.dtype)
        r_out[:, sl] = r.astype(r_out.dtype)
        k_out[:, sl] = k2.astype(k_out.dtype)
        v_out[:, sl] = v.astype(v_out.dtype)
        kkn_out[:, sl] = kk.astype(kkn_out.dtype)
        a_out[:, sl] = a.astype(a_out.dtype)
    carry_ref[0:1, :] = u_ref[tb - 1:tb, :]


def _hi_lo_hi(w):
    hi, lo = _split2(w)
    return jnp.concatenate([hi, lo, hi], axis=0)


def rwkv_prep(u, mu, w0, w2, a0, a2, g2, k_k, k_a, r_k, tb=128):
    T = u.shape[0]
    W = RWKV_WIDTH
    w2c, a2c, g2c = _hi_lo_hi(w2), _hi_lo_hi(a2), _hi_lo_hi(g2)
    row = lambda n: pl.BlockSpec((1, n), lambda i: (0, 0))
    full = lambda a: pl.BlockSpec(a.shape, lambda i: (0, 0))
    out_spec = pl.BlockSpec((tb, W), lambda i: (i, 0))
    return pl.pallas_call(
        _rwkv_prep_kernel,
        out_shape=[jax.ShapeDtypeStruct((T, W), F32 if i == 1 else BF16) for i in range(8)],
        grid=(T // tb,),
        in_specs=[pl.BlockSpec((tb, RWKV_PAD), lambda i: (i, 0)), row(RWKV_PAD), row(W), full(w2c),
                  row(W), full(a2c), full(g2c), row(W), row(W), row(W)],
        out_specs=[out_spec] * 8,
        scratch_shapes=[pltpu.VMEM((8, RWKV_PAD), F32)],
        compiler_params=_params(("arbitrary",)),
        name="rwkv_prep",
    )(u, mu.reshape(1, -1), w0.reshape(1, W), w2c, a0.reshape(1, W), a2c, g2c,
      k_k.reshape(1, W), k_a.reshape(1, W), r_k.reshape(1, W))


def _rwkv_chunk(ins, states, consts):
    tril_l, m0, strict, incl, eye = consts
    P = range(len(ins))
    L = ins[0][0].shape[0]
    n2 = 2 * L

    def stack(x):
        return jnp.concatenate([jnp.where(m0, x, 0.0), jnp.where(m0, 0.0, x)], axis=0)

    c = [_dot_exact_lhs(tril_l, ins[p][1]) for p in P]
    c_last = [c[p][L - 1:L, :] for p in P]
    lhs, rhs, r_s, a_s, v_s, lhs2 = [], [], [], [], [], []
    for p in P:
        r, lw, k, v, kk, a = ins[p]
        n_in = jnp.exp(-c[p])
        to_end = jnp.exp(c_last[p] - c[p])
        a_t = -kk * jnp.exp(c[p] - lw)
        b_raw = kk * a
        b_t = b_raw * n_in
        k_t = k * n_in
        a_s.append(stack(a_t))
        r_s.append(stack(r * jnp.exp(c[p])))
        v_s.append(stack(v))
        lhs.append(jnp.concatenate([a_s[p], r_s[p]], axis=0))
        rhs.append(jnp.concatenate([b_t, b_t, k_t, k_t], axis=0))
        lhs2.append(jnp.concatenate([stack(b_raw * to_end), stack(k * to_end)], axis=0))
    big = [_bdot_nt(lhs[p], rhs[p]) for p in P]
    a_ab = [jnp.where(strict, big[p][0:n2, 0:n2], 0.0) for p in P]
    a_ak = [jnp.where(strict, big[p][0:n2, n2:2 * n2], 0.0) for p in P]
    m_rb = [jnp.where(incl, big[p][n2:2 * n2, 0:n2], 0.0) for p in P]
    m_rk = [jnp.where(incl, big[p][n2:2 * n2, n2:2 * n2], 0.0) for p in P]

    t_inv = [eye + a_ab[p] for p in P]
    pw = [_bdot(a_ab[p], a_ab[p]) for p in P]
    av = [_bdot(a_ak[p], v_s[p]) for p in P]
    steps = (L - 1).bit_length() - 1
    for s in range(steps):
        if s < steps - 1:
            res = [_bdot(jnp.concatenate([t_inv[p], pw[p]], axis=0), pw[p]) for p in P]
            t_inv = [t_inv[p] + res[p][0:n2] for p in P]
            pw = [res[p][n2:2 * n2] for p in P]
        else:
            res = [_bdot(t_inv[p], pw[p]) for p in P]
            t_inv = [t_inv[p] + res[p] for p in P]

    sol = [_bdot(t_inv[p], jnp.concatenate([a_s[p], av[p]], axis=1)) for p in P]
    qy = [_bdot(m_rb[p], sol[p]) for p in P]
    yv = [_bdot(m_rk[p], v_s[p]) for p in P]
    gh = [_bdot_tn(lhs2[p], jnp.concatenate(
        [sol[p], jnp.concatenate([jnp.zeros_like(v_s[p]), v_s[p]], axis=1)], axis=0)) for p in P]
    g_mm = [gh[p][:, 0:LANES] for p in P]
    h_mm = [gh[p][:, LANES:2 * LANES] for p in P]
    outs = []
    for p in P:
        q_s = r_s[p] + qy[p][:, 0:LANES]
        y_s = qy[p][:, LANES:2 * LANES] + yv[p]
        q_h = q_s[0:L] + q_s[L:n2]
        y_in = y_s[0:L] + y_s[L:n2]
        g_mat = g_mm[p] + jnp.where(
            eye > 0, jnp.broadcast_to(jnp.exp(c_last[p]), (LANES, LANES)), 0.0)
        y = _bdot(q_h, states[p]) + y_in
        s_new = _bdot(g_mat, states[p]) + h_mm[p]
        outs.append((y, s_new))
    return outs


def _rwkv_scan_kernel(r_ref, lw_ref, k_ref, v_ref, kk_ref, a_ref, g_ref, bonus_ref, lnw_ref, lnb_ref,
                      y_ref, s_ref, yp_ref, *, pairs):
    c = pl.program_id(1)
    L = r_ref.shape[0]
    n2 = 2 * L

    @pl.when(c == 0)
    def _():
        s_ref[...] = jnp.zeros_like(s_ref)
        yp_ref[...] = jnp.zeros_like(yp_ref)

    ti = lax.broadcasted_iota(jnp.int32, (L, L), 0)
    tj = lax.broadcasted_iota(jnp.int32, (L, L), 1)
    tril_l = jnp.where(ti >= tj, 1.0, 0.0).astype(BF16)
    m0 = lax.broadcasted_iota(jnp.int32, (L, LANES), 1) < RWKV_HEAD
    si = lax.broadcasted_iota(jnp.int32, (n2, n2), 0)
    sj = lax.broadcasted_iota(jnp.int32, (n2, n2), 1)
    same = (si // L) == (sj // L)
    strict = same & ((sj % L) < (si % L))
    incl = same & ((sj % L) <= (si % L))
    eye = jnp.where(si == sj, 1.0, 0.0).astype(F32)
    consts = (tril_l, m0, strict, incl, eye)

    def half_sums(x):
        s0 = jnp.sum(jnp.where(m0, x, 0.0), axis=-1, keepdims=True)
        s1 = jnp.sum(jnp.where(m0, 0.0, x), axis=-1, keepdims=True)
        return jnp.where(m0, s0, s1)

    sls = [slice(p * LANES, (p + 1) * LANES) for p in range(pairs)]
    ins = [tuple(ref[:, sl].astype(F32) for ref in (r_ref, lw_ref, k_ref, v_ref, kk_ref, a_ref))
           for sl in sls]
    outs = _rwkv_chunk(ins, [s_ref[p] for p in range(pairs)], consts)
    cur = c % 2
    for p in range(pairs):
        yp_ref[cur, p] = outs[p][0]
        s_ref[p] = outs[p][1]

    inv_n = 1.0 / RWKV_HEAD
    for p in range(pairs):
        sl = sls[p]
        y = yp_ref[1 - cur, p]
        yc = y - half_sums(y) * inv_n
        var = half_sums(yc * yc) * inv_n
        out = yc * lax.rsqrt(var + RWKV_LN_EPS) * lnw_ref[:, sl] + lnb_ref[:, sl] + bonus_ref[:, sl].astype(F32)
        y_ref[:, sl] = (out * g_ref[:, sl].astype(F32)).astype(y_ref.dtype)


def rwkv_scan(r, lw, k, v, kk, a, g, bonus, ln_w, ln_b, pairs=12):
    T, W = r.shape
    L = RWKV_CHUNK
    wb = pairs * LANES
    nc = T // L
    spec = pl.BlockSpec((L, wb), lambda p, c: (jnp.minimum(c, nc - 1), p))
    late = pl.BlockSpec((L, wb), lambda p, c: (jnp.maximum(c - 1, 0), p))
    row = pl.BlockSpec((1, wb), lambda p, c: (0, p))
    return pl.pallas_call(
        functools.partial(_rwkv_scan_kernel, pairs=pairs),
        out_shape=jax.ShapeDtypeStruct((T, W), BF16),
        grid=(W // wb, nc + 1),
        in_specs=[spec] * 6 + [late, late, row, row],
        out_specs=late,
        scratch_shapes=[pltpu.VMEM((pairs, LANES, LANES), F32), pltpu.VMEM((2, pairs, L, LANES), F32)],
        compiler_params=_params(("parallel", "arbitrary")),
        name="rwkv_scan",
    )(r, lw, k, v, kk, a, g, bonus, ln_w.reshape(1, W), ln_b.reshape(1, W))


def rwkv_mixer(u, mu, w0, w2, a0, a2, g2, k_k, k_a, r_k, ln_w, ln_b):
    r, lw, k, v, kk, a, g, bonus = rwkv_prep(u, mu, w0, w2, a0, a2, g2, k_k, k_a, r_k)
    return rwkv_scan(r, lw, k, v, kk, a, g, bonus, ln_w, ln_b)


def _ssd_kernel(p_ref, cw_ref, cb_ref, dtb_ref, alog_ref, dsk_ref, ng_ref, o_ref,
                ext_ref, xbc_ref, st_ref, y_scr):
    c = pl.program_id(0)
    L = SSM_CHUNK
    X0 = SSM_INNER
    HP = 2 * 64

    @pl.when(c == 0)
    def _():
        ext_ref[0:8, :] = jnp.zeros((8, SSM_CONV_DIM), F32)
        st_ref[...] = jnp.zeros_like(st_ref)

    ext_ref[8:8 + L, :] = p_ref[:, X0:X0 + SSM_CONV_DIM]
    CW = 4 * LANES
    for cc in range(SSM_CONV_DIM // CW):
        sl = slice(cc * CW, (cc + 1) * CW)
        conv = jnp.broadcast_to(cb_ref[:, sl], (L, CW))
        for kk in range(SSM_CONV):
            conv = conv + cw_ref[kk:kk + 1, sl] * ext_ref[pl.ds(8 - (SSM_CONV - 1) + kk, L), sl]
        xbc_ref[:, sl] = _silu(conv)
    ext_ref[0:8, :] = p_ref[L - 8:L, X0:X0 + SSM_CONV_DIM]

    dt = _softplus(p_ref[:, X0 + SSM_CONV_DIM:X0 + SSM_CONV_DIM + LANES] + dtb_ref[...])
    d_a = dt * (-jnp.exp(alog_ref[...]))
    ti = lax.broadcasted_iota(jnp.int32, (L, L), 0)
    tj = lax.broadcasted_iota(jnp.int32, (L, L), 1)
    causal = ti >= tj
    a_cum = _dot_exact_lhs(jnp.where(causal, 1.0, 0.0).astype(BF16), d_a)
    a_cum_t = a_cum.T
    dt_t = dt.T
    lane = lax.broadcasted_iota(jnp.int32, (L, HP), 1)
    first = lane < 64

    for g in range(SSM_GROUPS):
        b_m = xbc_ref[:,SSM_INNER + g * SSM_STATE:SSM_INNER + (g + 1) * SSM_STATE]
        c_m = xbc_ref[:,SSM_INNER + SSM_GROUPS * SSM_STATE + g * SSM_STATE:
                  SSM_INNER + SSM_GROUPS * SSM_STATE + (g + 1) * SSM_STATE]
        cb = _bdot_nt(c_m, b_m)
        b_t = b_m.T
        for j in range(3):
            pair = g * 3 + j
            xs = xbc_ref[:,pair * HP:(pair + 1) * HP]
            yd, st, ea, cd = [], [], [], []
            for hh in (2 * pair, 2 * pair + 1):
                col = a_cum[:, hh:hh + 1]
                row = a_cum_t[hh:hh + 1, :]
                dtr = dt_t[hh:hh + 1, :]
                last = a_cum[L - 1:L, hh:hh + 1]
                lmat = jnp.exp(jnp.where(causal, col - row, -jnp.inf))
                yd.append(_bdot(cb * lmat * dtr, xs))
                st.append(_bdot(b_t * (jnp.exp(last - row) * dtr), xs))
                ea.append(jnp.exp(col))
                cd.append(jnp.exp(last))
            s_in = st_ref[pair]
            y = jnp.where(first, yd[0], yd[1])
            y = y + _bdot(c_m, s_in) * jnp.where(first, ea[0], ea[1])
            st_ref[pair] = s_in * jnp.where(first, cd[0], cd[1]) + jnp.where(first, st[0], st[1])
            y_scr[:, pair * HP:(pair + 1) * HP] = y + xs * dsk_ref[:, pair * HP:(pair + 1) * HP]

    gw = SSM_INNER // SSM_GROUPS
    for g in range(SSM_GROUPS):
        sl = slice(g * gw, (g + 1) * gw)
        yg = y_scr[:, sl] * _silu(p_ref[:, sl])
        ms = jnp.mean(yg * yg, axis=-1, keepdims=True)
        o_ref[:, sl] = (yg * lax.rsqrt(ms + SSM_NORM_EPS) * ng_ref[:, sl]).astype(o_ref.dtype)


def mamba2_mixer(u, conv_w, conv_b, dt_bias, a_log, d_skip, norm_g):
    T = u.shape[0]
    L = SSM_CHUNK
    padh = lambda x: jnp.pad(x, (0, LANES - SSM_HEADS)).reshape(1, LANES)
    full = lambda a: pl.BlockSpec(a.shape, lambda i: (0,) * a.ndim)
    args = (u, conv_w, conv_b.reshape(1, -1), padh(dt_bias), padh(a_log),
            jnp.repeat(d_skip, 64).reshape(1, SSM_INNER), norm_g.reshape(1, SSM_INNER))
    return pl.pallas_call(
        _ssd_kernel,
        out_shape=jax.ShapeDtypeStruct((T, SSM_INNER), BF16),
        grid=(T // L,),
        in_specs=[pl.BlockSpec((L, SSM_PAD), lambda i: (i, 0))] + [full(a) for a in args[1:]],
        out_specs=pl.BlockSpec((L, SSM_INNER), lambda i: (i, 0)),
        scratch_shapes=[pltpu.VMEM((L + 8, SSM_CONV_DIM), F32),
                        pltpu.VMEM((L, SSM_CONV_DIM), F32),
                        pltpu.VMEM((SSM_HEADS // 2, SSM_STATE, LANES), F32),
                        pltpu.VMEM((L, SSM_INNER), F32)],
        compiler_params=_params(("arbitrary",)),
        name="ssd",
    )(*args)


def kernel(x, norm_mix_g, w_in, rwkv_mu, rwkv_w0, rwkv_w2, rwkv_a0, rwkv_a2, rwkv_g2, rwkv_k_k, rwkv_k_a, rwkv_r_k, rwkv_ln_w, rwkv_ln_b, ssm_conv_w, ssm_conv_b, ssm_dt_bias, ssm_a_log, ssm_d, ssm_norm_g, p_attn, p_rwkv, p_ssm, w_out, norm_ffn_g, w_ffn_gate, w_ffn_up, w_ffn_down, norm_final_g):
    B, T, D = x.shape
    assert B == 1 and D == D_MODEL
    xt = x.reshape(T, D)
    o_r = ATT_COLS
    o_s = ATT_COLS + RWKV_COLS
    o_g = ATT_COLS + RWKV_COLS + SSM_COLS
    TM, TN = 1024, 512
    p_attn_b, p_rwkv_b, p_ssm_b = p_attn.astype(BF16), p_rwkv.astype(BF16), p_ssm.astype(BF16)
    w_out_b, w_down_b = w_out.astype(BF16), w_ffn_down.astype(BF16)
    wt_in = jnp.swapaxes(w_in, 1, 2).astype(BF16)
    for l in range(DEPTH):
        h = rms_norm(xt, norm_mix_g[l], BF16)
        qkv = matmul_nt(h, wt_in, l, 0, ATT_COLS, F32, TM, 2 * TN, "proj_att")
        u_rwkv = matmul_nt(h, wt_in, l, o_r, RWKV_PAD, F32, TM, 2 * TN, "proj_rwkv")
        u_ssm = matmul_nt(h, wt_in, l, o_s, SSM_PAD, F32, TM, TN, "proj_ssm")
        gates = matmul_nt(h, wt_in, l, o_g, GATE_COLS, BF16, TM, 2 * TN, "proj_gate")

        y_att = attention_mixer(qkv)
        g2 = jnp.pad(rwkv_g2[l], ((0, RWKV_G_PAD - RWKV_LORA_G), (0, 0)))
        mu = jnp.pad(rwkv_mu[l], (0, RWKV_PAD - RWKV_COLS))
        y_rwkv = rwkv_mixer(u_rwkv, mu, rwkv_w0[l], rwkv_w2[l], rwkv_a0[l], rwkv_a2[l], g2,
                            rwkv_k_k[l], rwkv_k_a[l], rwkv_r_k[l].reshape(-1), rwkv_ln_w[l], rwkv_ln_b[l])
        y_ssm = mamba2_mixer(u_ssm, ssm_conv_w[l], ssm_conv_b[l], ssm_dt_bias[l], ssm_a_log[l],
                             ssm_d[l], ssm_norm_g[l])

        merged = merge_proj(y_att, y_rwkv, y_ssm, p_attn_b, p_rwkv_b, p_ssm_b, gates, TM, TN, l)
        xt = matmul_residual(merged, w_out_b, xt, TM, TN, "w_out", l)

        h2 = rms_norm(xt, norm_ffn_g[l], BF16)
        act = ffn_up_ws(h2, w_ffn_gate, w_ffn_up, l, 2 * TM, 256)
        xt = matmul_residual(act, w_down_b, xt, TM // 2, TN, "ffn_down", l)
    out = rms_norm(xt, norm_final_g, F32)
    return out.reshape(B, T, D)
```

```python
import functools

import jax
import jax.numpy as jnp
from jax import lax
from jax.experimental import pallas as pl
from jax.experimental.pallas import tpu as pltpu

F32 = jnp.float32
BF16 = jnp.bfloat16

D_MODEL = 4096
DEPTH = 2
NORM_EPS = 1e-6
ATT_GROUPS = ((128, 1), (512, 4), (2048, 16))
ATT_HEADS = 8
ATT_DIM = 128
ATT_WIDTH = 3072
ATT_OUT = 1024
ATT_COLS = 9216
RWKV_WIDTH = 3072
RWKV_HEAD = 64
RWKV_LORA_W = 128
RWKV_LORA_A = 128
RWKV_LORA_G = 480
RWKV_COLS = 9952
RWKV_LN_EPS = 64e-5
SSM_INNER = 3072
SSM_HEADS = 48
SSM_GROUPS = 8
SSM_STATE = 128
SSM_CONV = 4
SSM_CHUNK = 128
SSM_CONV_DIM = 5120
SSM_COLS = 8240
SSM_NORM_EPS = 1e-5
FFN_HIDDEN = 11008
GATE_COLS = 3 * D_MODEL

RWKV_PAD = 10240
RWKV_G_PAD = 512
SSM_PAD = 8704

LANES = 128
RWKV_CHUNK = 64
VMEM_LIMIT = 56 * 1024 * 1024
FFN_UP_VMEM_LIMIT = 61 * 1024 * 1024


def _params(sem, vmem_limit=VMEM_LIMIT):
    return pltpu.CompilerParams(dimension_semantics=sem, vmem_limit_bytes=vmem_limit)


def _bdot(a, b):
    return jnp.dot(a.astype(BF16), b.astype(BF16), preferred_element_type=F32)


def _bdot_nt(a, b):
    return lax.dot_general(a.astype(BF16), b.astype(BF16), (((1,), (1,)), ((), ())),
                           preferred_element_type=F32)


def _bdot_tn(a, b):
    return lax.dot_general(a.astype(BF16), b.astype(BF16), (((0,), (0,)), ((), ())),
                           preferred_element_type=F32)


def _split2(x):
    hi = x.astype(BF16)
    lo = (x - hi.astype(F32)).astype(BF16)
    return hi, lo


def _split3(x):
    hi = x.astype(BF16)
    r1 = x - hi.astype(F32)
    mid = r1.astype(BF16)
    lo = (r1 - mid.astype(F32)).astype(BF16)
    return hi, mid, lo


def _dot_exact_lhs(m_bf16, x):
    hi, mid, lo = _split3(x)
    d = functools.partial(jnp.dot, preferred_element_type=F32)
    return d(m_bf16, hi) + d(m_bf16, mid) + d(m_bf16, lo)


NEG_LOG2E = -1.4426950408889634


def _sigmoid(x):
    return 1.0 / (1.0 + jnp.exp2(x * NEG_LOG2E))


def _silu(x):
    return x * _sigmoid(x)


def _softplus(x):
    return jnp.maximum(x, 0.0) + jnp.log(1.0 + jnp.exp2(jnp.abs(x) * NEG_LOG2E))


def _rms_kernel(x_ref, g_ref, o_ref):
    x = x_ref[...]
    ms = jnp.mean(x * x, axis=-1, keepdims=True)
    o_ref[...] = (x * lax.rsqrt(ms + NORM_EPS) * g_ref[...]).astype(o_ref.dtype)


def rms_norm(x, g, out_dtype, tr=256):
    T, D = x.shape
    return pl.pallas_call(
        _rms_kernel,
        out_shape=jax.ShapeDtypeStruct((T, D), out_dtype),
        grid=(T // tr,),
        in_specs=[pl.BlockSpec((tr, D), lambda i: (i, 0)),
                  pl.BlockSpec((1, D), lambda i: (0, 0))],
        out_specs=pl.BlockSpec((tr, D), lambda i: (i, 0)),
        compiler_params=_params(("parallel",)),
        name="rms_norm",
    )(x, g.reshape(1, D))


def _mm_nt_kernel(a_ref, w_ref, o_ref):
    o_ref[...] = lax.dot_general(a_ref[...], w_ref[0], (((1,), (1,)), ((), ())),
                                 preferred_element_type=F32).astype(o_ref.dtype)


def matmul_nt(a, wt, layer, row0, n_out, out_dtype, tm, tn, name):
    M, K = a.shape
    assert row0 % 16 == 0
    return pl.pallas_call(
        _mm_nt_kernel,
        out_shape=jax.ShapeDtypeStruct((M, n_out), out_dtype),
        grid=(M // tm, n_out // tn),
        in_specs=[pl.BlockSpec((tm, K), lambda i, j: (i, 0)),
                  pl.BlockSpec((pl.Element(1), pl.Element(tn), pl.Element(K)),
                               lambda i, j: (layer, pl.multiple_of(row0 + j * tn, 16), 0))],
        out_specs=pl.BlockSpec((tm, tn), lambda i, j: (i, j)),
        compiler_params=_params(("parallel", "arbitrary")),
        name=name,
    )(a, wt)


def _w_spec(w, layer, tn, col):
    if w.ndim == 2:
        return pl.BlockSpec((w.shape[0], tn), lambda i, j: (0, col(j)))
    return pl.BlockSpec((None, w.shape[1], tn), lambda i, j: (layer, 0, col(j)))


def _ffn_up_ws_kernel(h_ref, wg_ref, wu_ref, o_ref, wgb_ref, wub_ref):
    @pl.when(pl.program_id(1) == 0)
    def _():
        wgb_ref[...] = wg_ref[...].astype(BF16)
        wub_ref[...] = wu_ref[...].astype(BF16)

    h = h_ref[...]
    gate = jnp.dot(h, wgb_ref[...], preferred_element_type=F32)
    up = jnp.dot(h, wub_ref[...], preferred_element_type=F32)
    o_ref[...] = (_silu(gate) * up).astype(o_ref.dtype)


def ffn_up_ws(h, wg_stack, wu_stack, layer, tm, tn):
    M, K = h.shape
    N = wg_stack.shape[2]
    wspec = pl.BlockSpec((None, K, tn), lambda j, i: (layer, 0, j))
    return pl.pallas_call(
        _ffn_up_ws_kernel,
        out_shape=jax.ShapeDtypeStruct((M, N), BF16),
        grid=(N // tn, M // tm),
        in_specs=[pl.BlockSpec((tm, K), lambda j, i: (i, 0)), wspec, wspec],
        out_specs=pl.BlockSpec((tm, tn), lambda j, i: (i, j)),
        scratch_shapes=[pltpu.VMEM((K, tn), BF16)] * 2,
        compiler_params=_params(("arbitrary", "arbitrary"), FFN_UP_VMEM_LIMIT),
        name="ffn_up",
    )(h, wg_stack, wu_stack)


def _mm_res_kernel(a_ref, w_ref, x_ref, o_ref):
    o_ref[...] = x_ref[...] + jnp.dot(a_ref[...], w_ref[...], preferred_element_type=F32)


def matmul_residual(a, w, x, tm, tn, name, layer=0):
    M, K = a.shape
    N = w.shape[-1]
    return pl.pallas_call(
        _mm_res_kernel,
        out_shape=jax.ShapeDtypeStruct((M, N), F32),
        grid=(M // tm, N // tn),
        in_specs=[pl.BlockSpec((tm, K), lambda i, j: (i, 0)),
                  _w_spec(w, layer, tn, lambda j: j),
                  pl.BlockSpec((tm, tn), lambda i, j: (i, j))],
        out_specs=pl.BlockSpec((tm, tn), lambda i, j: (i, j)),
        compiler_params=_params(("parallel", "arbitrary")),
        name=name,
    )(a, w, x)


def _merge_proj_kernel(ya_ref, yr_ref, ys_ref, pa_ref, pr_ref, ps_ref,
                       ga_ref, gr_ref, gs_ref, o_ref):
    def sig(ref):
        return _sigmoid(ref[...].astype(F32))

    acc = sig(ga_ref) * jnp.dot(ya_ref[...], pa_ref[...], preferred_element_type=F32)
    acc += sig(gr_ref) * jnp.dot(yr_ref[...], pr_ref[...], preferred_element_type=F32)
    acc += sig(gs_ref) * jnp.dot(ys_ref[...], ps_ref[...], preferred_element_type=F32)
    o_ref[...] = acc.astype(o_ref.dtype)


def merge_proj(y_att, y_rwkv, y_ssm, p_att, p_rwkv, p_ssm, gates, tm, tn, layer):
    M = y_att.shape[0]
    N = p_att.shape[-1]
    nb = N // tn
    return pl.pallas_call(
        _merge_proj_kernel,
        out_shape=jax.ShapeDtypeStruct((M, N), BF16),
        grid=(M // tm, nb),
        in_specs=[pl.BlockSpec((tm, y_att.shape[1]), lambda i, j: (i, 0)),
                  pl.BlockSpec((tm, y_rwkv.shape[1]), lambda i, j: (i, 0)),
                  pl.BlockSpec((tm, y_ssm.shape[1]), lambda i, j: (i, 0)),
                  _w_spec(p_att, layer, tn, lambda j: j),
                  _w_spec(p_rwkv, layer, tn, lambda j: j),
                  _w_spec(p_ssm, layer, tn, lambda j: j),
                  pl.BlockSpec((tm, tn), lambda i, j: (i, j)),
                  pl.BlockSpec((tm, tn), lambda i, j: (i, j + nb)),
                  pl.BlockSpec((tm, tn), lambda i, j: (i, j + 2 * nb))],
        out_specs=pl.BlockSpec((tm, tn), lambda i, j: (i, j)),
        compiler_params=_params(("parallel", "arbitrary")),
        name="merge_proj",
    )(y_att, y_rwkv, y_ssm, p_att, p_rwkv, p_ssm, gates, gates, gates)


ATT_WIN = 2048


def _rows(start, size, stride):
    return pl.ds(start, size) if stride == 1 else pl.ds(start, size, stride=stride)


def _attn_kernel(*refs):
    n_g = len(ATT_GROUPS)
    ins = refs[:5 * n_g]
    y_ref = refs[5 * n_g]
    scr = refs[5 * n_g + 1:]
    kext, vext = scr[0:n_g], scr[n_g:2 * n_g]
    o_scr, l_scr = scr[2 * n_g], scr[2 * n_g + 1]
    w = pl.program_id(1)
    blk = ATT_DIM
    qi = lax.broadcasted_iota(jnp.int32, (blk, 2 * blk), 0)
    kj = lax.broadcasted_iota(jnp.int32, (blk, 2 * blk), 1)
    dist = qi + blk - kj
    scale = ATT_DIM ** -0.5

    for g, (window, d) in enumerate(ATT_GROUPS):
        q_ref, k_ref, v_ref, kp_ref, vp_ref = ins[5 * g:5 * g + 5]
        halo = d * blk
        kext[g][0:halo, :] = kp_ref[...]
        kext[g][halo:halo + ATT_WIN, :] = k_ref[...]
        vext[g][0:halo, :] = vp_ref[...]
        vext[g][halo:halo + ATT_WIN, :] = v_ref[...]
        band = (dist >= 0) & (dist <= window // d)
        band_first = band & jnp.logical_or(w > 0, kj >= blk)
        for res in range(d):
            for m in range(ATT_WIN // halo):
                row0 = res + halo * m
                q = q_ref[_rows(row0, blk, d), :].astype(BF16)
                k = kext[g][_rows(row0, 2 * blk, d), :].astype(BF16)
                v = vext[g][_rows(row0, 2 * blk, d), :].astype(BF16)
                s = lax.dot_general(q, k, (((1,), (1,)), ((), ())), preferred_element_type=F32) * scale
                s = jnp.where(band if m > 0 else band_first, s, -jnp.inf)
                mx = jnp.max(s, axis=-1, keepdims=True)
                p = jnp.exp(s - mx)
                den = jnp.sum(p, axis=-1, keepdims=True)
                o = jnp.dot((p / den).astype(BF16), v, preferred_element_type=F32)
                o_scr[g, _rows(row0, blk, d), :] = o
                l_scr[g, _rows(row0, blk, d), :] = jnp.broadcast_to(mx + jnp.log(den), (blk, blk))

    rc = 256
    for c in range(ATT_WIN // rc):
        rows = slice(c * rc, (c + 1) * rc)
        la, lb, lc = l_scr[0, rows, :], l_scr[1, rows, :], l_scr[2, rows, :]
        mx = jnp.maximum(jnp.maximum(la, lb), lc)
        ea, eb, ec = jnp.exp(la - mx), jnp.exp(lb - mx), jnp.exp(lc - mx)
        y = (ea * o_scr[0, rows, :] + eb * o_scr[1, rows, :] + ec * o_scr[2, rows, :]) / (ea + eb + ec)
        y_ref[rows, :] = y.astype(y_ref.dtype)


def attention_mixer(qkv):
    T = qkv.shape[0]
    blk = ATT_DIM
    n_g = len(ATT_GROUPS)
    sect = n_g * ATT_HEADS
    in_specs, scratch_k, scratch_v = [], [], []
    for g, (window, d) in enumerate(ATT_GROUPS):
        per_win = ATT_WIN // (d * blk)
        for s in range(3):
            in_specs.append(pl.BlockSpec((ATT_WIN, blk),
                                         lambda h, w, s=s, g=g: (w, s * sect + g * ATT_HEADS + h)))
        for s in (1, 2):
            in_specs.append(pl.BlockSpec(
                (d * blk, blk),
                lambda h, w, s=s, g=g, per_win=per_win: (jnp.maximum(w * per_win - 1, 0),
                                                         s * sect + g * ATT_HEADS + h)))
        scratch_k.append(pltpu.VMEM((ATT_WIN + d * blk, blk), F32))
        scratch_v.append(pltpu.VMEM((ATT_WIN + d * blk, blk), F32))
    return pl.pallas_call(
        _attn_kernel,
        out_shape=jax.ShapeDtypeStruct((T, ATT_OUT), BF16),
        grid=(ATT_HEADS, T // ATT_WIN),
        in_specs=in_specs,
        out_specs=pl.BlockSpec((ATT_WIN, blk), lambda h, w: (w, h)),
        scratch_shapes=scratch_k + scratch_v + [pltpu.VMEM((n_g, ATT_WIN, blk), F32)] * 2,
        compiler_params=_params(("parallel", "arbitrary")),
        name="attention",
    )(*([qkv] * (5 * n_g)))


def _rwkv_prep_kernel(u_ref, mu_ref, w0_ref, w2_ref, a0_ref, a2_ref, g2_ref, kk_ref, ka_ref, rk_ref,
                      r_out, lw_out, k_out, v_out, kkn_out, a_out, g_out, bonus_out, carry_ref):
    i = pl.program_id(0)
    tb = u_ref.shape[0]
    W = RWKV_WIDTH
    CW = 2 * LANES

    @pl.when(i == 0)
    def _():
        carry_ref[...] = jnp.zeros_like(carry_ref)

    def mixed(lo, width):
        sl = slice(lo, lo + width)
        u = u_ref[:, sl]
        row = lax.broadcasted_iota(jnp.int32, u.shape, 0)
        u_prev = jnp.where(row == 0, jnp.broadcast_to(carry_ref[0:1, sl], u.shape), pltpu.roll(u, 1, 0))
        return u + (u_prev - u) * mu_ref[:, sl]

    def dot3(a_split, b_ref, sl):
        ah, al = a_split
        return jnp.dot(jnp.concatenate([ah, ah, al], axis=1), b_ref[:, sl], preferred_element_type=F32)

    m0 = lax.broadcasted_iota(jnp.int32, (tb, LANES), 1) < RWKV_HEAD

    def head_sums(x):
        cols = []
        for c in range(CW // LANES):
            xb = x[:, c * LANES:(c + 1) * LANES]
            s0 = jnp.sum(jnp.where(m0, xb, 0.0), axis=-1, keepdims=True)
            s1 = jnp.sum(jnp.where(m0, 0.0, xb), axis=-1, keepdims=True)
            cols.append(jnp.where(m0, s0, s1))
        return jnp.concatenate(cols, axis=1)

    o = 3 * W
    th = _split2(jnp.tanh(mixed(o, RWKV_LORA_W)))
    xa = _split2(mixed(o + RWKV_LORA_W, RWKV_LORA_A))
    sg = _split2(_sigmoid(mixed(o + RWKV_LORA_W + RWKV_LORA_A, RWKV_G_PAD)))
    for cc in range(W // CW):
        sl = slice(cc * CW, (cc + 1) * CW)
        r = mixed(cc * CW, CW)
        k = mixed(W + cc * CW, CW)
        v = mixed(2 * W + cc * CW, CW)
        w_log = -_softplus(-(w0_ref[:, sl] + dot3(th, w2_ref, sl))) - 0.5
        lw_out[:, sl] = -jnp.exp(w_log)
        a = _sigmoid(a0_ref[:, sl] + dot3(xa, a2_ref, sl))
        g_out[:, sl] = dot3(sg, g2_ref, sl).astype(g_out.dtype)
        kk = k * kk_ref[:, sl]
        kk = kk / jnp.maximum(jnp.sqrt(head_sums(kk * kk)), 1e-12)
        k2 = k * (1.0 + (a - 1.0) * ka_ref[:, sl])
        bonus_out[:, sl] = (head_sums(r * k2 * rk_ref[:, sl]) * v).astype(bonus_out.dtype)
        r_out[:, sl] = r.astype(r_out.dtype)
        k_out[:, sl] = k2.astype(k_out.dtype)
        v_out[:, sl] = v.astype(v_out.dtype)
        kkn_out[:, sl] = kk.astype(kkn_out.dtype)
        a_out[:, sl] = a.astype(a_out.dtype)
    carry_ref[0:1, :] = u_ref[tb - 1:tb, :]


def _hi_lo_hi(w):
    hi, lo = _split2(w)
    return jnp.concatenate([hi, lo, hi], axis=0)


def rwkv_prep(u, mu, w0, w2, a0, a2, g2, k_k, k_a, r_k, tb=128):
    T = u.shape[0]
    W = RWKV_WIDTH
    w2c, a2c, g2c = _hi_lo_hi(w2), _hi_lo_hi(a2), _hi_lo_hi(g2)
    row = lambda n: pl.BlockSpec((1, n), lambda i: (0, 0))
    full = lambda a: pl.BlockSpec(a.shape, lambda i: (0, 0))
    out_spec = pl.BlockSpec((tb, W), lambda i: (i, 0))
    return pl.pallas_call(
        _rwkv_prep_kernel,
        out_shape=[jax.ShapeDtypeStruct((T, W), F32 if i == 1 else BF16) for i in range(8)],
        grid=(T // tb,),
        in_specs=[pl.BlockSpec((tb, RWKV_PAD), lambda i: (i, 0)), row(RWKV_PAD), row(W), full(w2c),
                  row(W), full(a2c), full(g2c), row(W), row(W), row(W)],
        out_specs=[out_spec] * 8,
        scratch_shapes=[pltpu.VMEM((8, RWKV_PAD), F32)],
        compiler_params=_params(("arbitrary",)),
        name="rwkv_prep",
    )(u, mu.reshape(1, -1), w0.reshape(1, W), w2c, a0.reshape(1, W), a2c, g2c,
      k_k.reshape(1, W), k_a.reshape(1, W), r_k.reshape(1, W))


def _rwkv_chunk(ins, states, consts):
    tril_l, m0, strict, incl, eye = consts
    P = range(len(ins))
    L = ins[0][0].shape[0]
    n2 = 2 * L

    def stack(x):
        return jnp.concatenate([jnp.where(m0, x, 0.0), jnp.where(m0, 0.0, x)], axis=0)

    c = [_dot_exact_lhs(tril_l, ins[p][1]) for p in P]
    c_last = [c[p][L - 1:L, :] for p in P]
    lhs, rhs, r_s, a_s, v_s, lhs2 = [], [], [], [], [], []
    for p in P:
        r, lw, k, v, kk, a = ins[p]
        n_in = jnp.exp(-c[p])
        to_end = jnp.exp(c_last[p] - c[p])
        a_t = -kk * jnp.exp(c[p] - lw)
        b_raw = kk * a
        b_t = b_raw * n_in
        k_t = k * n_in
        a_s.append(stack(a_t))
        r_s.append(stack(r * jnp.exp(c[p])))
        v_s.append(stack(v))
        lhs.append(jnp.concatenate([a_s[p], r_s[p]], axis=0))
        rhs.append(jnp.concatenate([b_t, b_t, k_t, k_t], axis=0))
        lhs2.append(jnp.concatenate([stack(b_raw * to_end), stack(k * to_end)], axis=0))
    big = [_bdot_nt(lhs[p], rhs[p]) for p in P]
    a_ab = [jnp.where(strict, big[p][0:n2, 0:n2], 0.0) for p in P]
    a_ak = [jnp.where(strict, big[p][0:n2, n2:2 * n2], 0.0) for p in P]
    m_rb = [jnp.where(incl, big[p][n2:2 * n2, 0:n2], 0.0) for p in P]
    m_rk = [jnp.where(incl, big[p][n2:2 * n2, n2:2 * n2], 0.0) for p in P]

    t_inv = [eye + a_ab[p] for p in P]
    pw = [_bdot(a_ab[p], a_ab[p]) for p in P]
    av = [_bdot(a_ak[p], v_s[p]) for p in P]
    steps = (L - 1).bit_length() - 1
    for s in range(steps):
        if s < steps - 1:
            res = [_bdot(jnp.concatenate([t_inv[p], pw[p]], axis=0), pw[p]) for p in P]
            t_inv = [t_inv[p] + res[p][0:n2] for p in P]
            pw = [res[p][n2:2 * n2] for p in P]
        else:
            res = [_bdot(t_inv[p], pw[p]) for p in P]
            t_inv = [t_inv[p] + res[p] for p in P]

    sol = [_bdot(t_inv[p], jnp.concatenate([a_s[p], av[p]], axis=1)) for p in P]
    qy = [_bdot(m_rb[p], sol[p]) for p in P]
    yv = [_bdot(m_rk[p], v_s[p]) for p in P]
    gh = [_bdot_tn(lhs2[p], jnp.concatenate(
        [sol[p], jnp.concatenate([jnp.zeros_like(v_s[p]), v_s[p]], axis=1)], axis=0)) for p in P]
    g_mm = [gh[p][:, 0:LANES] for p in P]
    h_mm = [gh[p][:, LANES:2 * LANES] for p in P]
    outs = []
    for p in P:
        q_s = r_s[p] + qy[p][:, 0:LANES]
        y_s = qy[p][:, LANES:2 * LANES] + yv[p]
        q_h = q_s[0:L] + q_s[L:n2]
        y_in = y_s[0:L] + y_s[L:n2]
        g_mat = g_mm[p] + jnp.where(
            eye > 0, jnp.broadcast_to(jnp.exp(c_last[p]), (LANES, LANES)), 0.0)
        y = _bdot(q_h, states[p]) + y_in
        s_new = _bdot(g_mat, states[p]) + h_mm[p]
        outs.append((y, s_new))
    return outs


def _rwkv_scan_kernel(r_ref, lw_ref, k_ref, v_ref, kk_ref, a_ref, g_ref, bonus_ref, lnw_ref, lnb_ref,
                      y_ref, s_ref, yp_ref, *, pairs):
    c = pl.program_id(1)
    L = r_ref.shape[0]
    n2 = 2 * L

    @pl.when(c == 0)
    def _():
        s_ref[...] = jnp.zeros_like(s_ref)
        yp_ref[...] = jnp.zeros_like(yp_ref)

    ti = lax.broadcasted_iota(jnp.int32, (L, L), 0)
    tj = lax.broadcasted_iota(jnp.int32, (L, L), 1)
    tril_l = jnp.where(ti >= tj, 1.0, 0.0).astype(BF16)
    m0 = lax.broadcasted_iota(jnp.int32, (L, LANES), 1) < RWKV_HEAD
    si = lax.broadcasted_iota(jnp.int32, (n2, n2), 0)
    sj = lax.broadcasted_iota(jnp.int32, (n2, n2), 1)
    same = (si // L) == (sj // L)
    strict = same & ((sj % L) < (si % L))
    incl = same & ((sj % L) <= (si % L))
    eye = jnp.where(si == sj, 1.0, 0.0).astype(F32)
    consts = (tril_l, m0, strict, incl, eye)

    def half_sums(x):
        s0 = jnp.sum(jnp.where(m0, x, 0.0), axis=-1, keepdims=True)
        s1 = jnp.sum(jnp.where(m0, 0.0, x), axis=-1, keepdims=True)
        return jnp.where(m0, s0, s1)

    sls = [slice(p * LANES, (p + 1) * LANES) for p in range(pairs)]
    ins = [tuple(ref[:, sl].astype(F32) for ref in (r_ref, lw_ref, k_ref, v_ref, kk_ref, a_ref))
           for sl in sls]
    outs = _rwkv_chunk(ins, [s_ref[p] for p in range(pairs)], consts)
    cur = c % 2
    for p in range(pairs):
        yp_ref[cur, p] = outs[p][0]
        s_ref[p] = outs[p][1]

    inv_n = 1.0 / RWKV_HEAD
    for p in range(pairs):
        sl = sls[p]
        y = yp_ref[1 - cur, p]
        yc = y - half_sums(y) * inv_n
        var = half_sums(yc * yc) * inv_n
        out = yc * lax.rsqrt(var + RWKV_LN_EPS) * lnw_ref[:, sl] + lnb_ref[:, sl] + bonus_ref[:, sl].astype(F32)
        y_ref[:, sl] = (out * g_ref[:, sl].astype(F32)).astype(y_ref.dtype)


def rwkv_scan(r, lw, k, v, kk, a, g, bonus, ln_w, ln_b, pairs=12):
    T, W = r.shape
    L = RWKV_CHUNK
    wb = pairs * LANES
    nc = T // L
    spec = pl.BlockSpec((L, wb), lambda p, c: (jnp.minimum(c, nc - 1), p))
    late = pl.BlockSpec((L, wb), lambda p, c: (jnp.maximum(c - 1, 0), p))
    row = pl.BlockSpec((1, wb), lambda p, c: (0, p))
    return pl.pallas_call(
        functools.partial(_rwkv_scan_kernel, pairs=pairs),
        out_shape=jax.ShapeDtypeStruct((T, W), BF16),
        grid=(W // wb, nc + 1),
        in_specs=[spec] * 6 + [late, late, row, row],
        out_specs=late,
        scratch_shapes=[pltpu.VMEM((pairs, LANES, LANES), F32), pltpu.VMEM((2, pairs, L, LANES), F32)],
        compiler_params=_params(("parallel", "arbitrary")),
        name="rwkv_scan",
    )(r, lw, k, v, kk, a, g, bonus, ln_w.reshape(1, W), ln_b.reshape(1, W))


def rwkv_mixer(u, mu, w0, w2, a0, a2, g2, k_k, k_a, r_k, ln_w, ln_b):
    r, lw, k, v, kk, a, g, bonus = rwkv_prep(u, mu, w0, w2, a0, a2, g2, k_k, k_a, r_k)
    return rwkv_scan(r, lw, k, v, kk, a, g, bonus, ln_w, ln_b)


def _ssd_chunk(c, p_ref, cw_ref, cb_ref, dtb_ref, alog_ref, dsk_ref, ng_ref, o_ref,
               ext_ref, xbc_ref, st_ref, y_scr):
    L = p_ref.shape[0]
    X0 = SSM_INNER
    HP = 2 * 64

    @pl.when(c == 0)
    def _():
        ext_ref[0:8, :] = jnp.zeros((8, SSM_CONV_DIM), F32)
        st_ref[...] = jnp.zeros_like(st_ref)

    ext_ref[8:8 + L, :] = p_ref[:, X0:X0 + SSM_CONV_DIM]
    CW = 4 * LANES
    for cc in range(SSM_CONV_DIM // CW):
        sl = slice(cc * CW, (cc + 1) * CW)
        conv = jnp.broadcast_to(cb_ref[:, sl], (L, CW))
        for kk in range(SSM_CONV):
            conv = conv + cw_ref[kk:kk + 1, sl] * ext_ref[pl.ds(8 - (SSM_CONV - 1) + kk, L), sl]
        xbc_ref[:, sl] = _silu(conv)
    ext_ref[0:8, :] = p_ref[L - 8:L, X0:X0 + SSM_CONV_DIM]

    dt = _softplus(p_ref[:, X0 + SSM_CONV_DIM:X0 + SSM_CONV_DIM + LANES] + dtb_ref[...])
    d_a = dt * (-jnp.exp(alog_ref[...]))
    ti = lax.broadcasted_iota(jnp.int32, (L, L), 0)
    tj = lax.broadcasted_iota(jnp.int32, (L, L), 1)
    causal = ti >= tj
    a_cum = _dot_exact_lhs(jnp.where(causal, 1.0, 0.0).astype(BF16), d_a)
    a_cum_t = a_cum.T
    dt_t = dt.T
    first = lax.broadcasted_iota(jnp.int32, (L, HP), 1) < 64
    first_n = lax.broadcasted_iota(jnp.int32, (SSM_STATE, HP), 1) < 64

    for g in range(SSM_GROUPS):
        b_m = xbc_ref[:,SSM_INNER + g * SSM_STATE:SSM_INNER + (g + 1) * SSM_STATE]
        c_m = xbc_ref[:,SSM_INNER + SSM_GROUPS * SSM_STATE + g * SSM_STATE:
                  SSM_INNER + SSM_GROUPS * SSM_STATE + (g + 1) * SSM_STATE]
        cb = _bdot_nt(c_m, b_m)
        b_t = b_m.T
        for j in range(3):
            pair = g * 3 + j
            xs = xbc_ref[:,pair * HP:(pair + 1) * HP]
            yd, st, ea, cd = [], [], [], []
            for hh in (2 * pair, 2 * pair + 1):
                col = a_cum[:, hh:hh + 1]
                row = a_cum_t[hh:hh + 1, :]
                dtr = dt_t[hh:hh + 1, :]
                last = a_cum[L - 1:L, hh:hh + 1]
                lmat = jnp.exp(jnp.where(causal, col - row, -jnp.inf))
                yd.append(_bdot(cb * lmat * dtr, xs))
                st.append(_bdot(b_t * (jnp.exp(last - row) * dtr), xs))
                ea.append(jnp.exp(col))
                cd.append(jnp.exp(last))
            s_in = st_ref[pair]
            y = jnp.where(first, yd[0], yd[1])
            y = y + _bdot(c_m, s_in) * jnp.where(first, ea[0], ea[1])
            st_ref[pair] = s_in * jnp.where(first_n, cd[0], cd[1]) + jnp.where(first_n, st[0], st[1])
            y_scr[:, pair * HP:(pair + 1) * HP] = y + xs * dsk_ref[:, pair * HP:(pair + 1) * HP]

    gw = SSM_INNER // SSM_GROUPS
    for g in range(SSM_GROUPS):
        sl = slice(g * gw, (g + 1) * gw)
        yg = y_scr[:, sl] * _silu(p_ref[:, sl])
        ms = jnp.mean(yg * yg, axis=-1, keepdims=True)
        o_ref[:, sl] = (yg * lax.rsqrt(ms + SSM_NORM_EPS) * ng_ref[:, sl]).astype(o_ref.dtype)


def _ssd_kernel(*refs):
    _ssd_chunk(pl.program_id(0), *refs)


def mamba2_mixer(u, conv_w, conv_b, dt_bias, a_log, d_skip, norm_g):
    T = u.shape[0]
    L = SSM_CHUNK
    padh = lambda x: jnp.pad(x, (0, LANES - SSM_HEADS)).reshape(1, LANES)
    full = lambda a: pl.BlockSpec(a.shape, lambda i: (0,) * a.ndim)
    args = (u, conv_w, conv_b.reshape(1, -1), padh(dt_bias), padh(a_log),
            jnp.repeat(d_skip, 64).reshape(1, SSM_INNER), norm_g.reshape(1, SSM_INNER))
    return pl.pallas_call(
        _ssd_kernel,
        out_shape=jax.ShapeDtypeStruct((T, SSM_INNER), BF16),
        grid=(T // L,),
        in_specs=[pl.BlockSpec((L, SSM_PAD), lambda i: (i, 0))] + [full(a) for a in args[1:]],
        out_specs=pl.BlockSpec((L, SSM_INNER), lambda i: (i, 0)),
        scratch_shapes=[pltpu.VMEM((L + 8, SSM_CONV_DIM), F32),
                        pltpu.VMEM((L, SSM_CONV_DIM), F32),
                        pltpu.VMEM((SSM_HEADS // 2, SSM_STATE, LANES), F32),
                        pltpu.VMEM((L, SSM_INNER), F32)],
        compiler_params=_params(("arbitrary",)),
        name="ssd",
    )(*args)


def kernel(x, norm_mix_g, w_in, rwkv_mu, rwkv_w0, rwkv_w2, rwkv_a0, rwkv_a2, rwkv_g2, rwkv_k_k, rwkv_k_a, rwkv_r_k, rwkv_ln_w, rwkv_ln_b, ssm_conv_w, ssm_conv_b, ssm_dt_bias, ssm_a_log, ssm_d, ssm_norm_g, p_attn, p_rwkv, p_ssm, w_out, norm_ffn_g, w_ffn_gate, w_ffn_up, w_ffn_down, norm_final_g):
    B, T, D = x.shape
    assert B == 1 and D == D_MODEL
    xt = x.reshape(T, D)
    o_r = ATT_COLS
    o_s = ATT_COLS + RWKV_COLS
    o_g = ATT_COLS + RWKV_COLS + SSM_COLS
    TM, TN = 1024, 512
    p_attn_b, p_rwkv_b, p_ssm_b = p_attn.astype(BF16), p_rwkv.astype(BF16), p_ssm.astype(BF16)
    w_out_b, w_down_b = w_out.astype(BF16), w_ffn_down.astype(BF16)
    wt_in = jnp.swapaxes(w_in, 1, 2).astype(BF16)
    for l in range(DEPTH):
        h = rms_norm(xt, norm_mix_g[l], BF16)
        qkv = matmul_nt(h, wt_in, l, 0, ATT_COLS, F32, TM, 2 * TN, "proj_att")
        u_rwkv = matmul_nt(h, wt_in, l, o_r, RWKV_PAD, F32, TM, 2 * TN, "proj_rwkv")
        u_ssm = matmul_nt(h, wt_in, l, o_s, SSM_PAD, F32, 2 * TM, TN, "proj_ssm")
        gates = matmul_nt(h, wt_in, l, o_g, GATE_COLS, BF16, TM, 2 * TN, "proj_gate")

        y_att = attention_mixer(qkv)
        g2 = jnp.pad(rwkv_g2[l], ((0, RWKV_G_PAD - RWKV_LORA_G), (0, 0)))
        mu = jnp.pad(rwkv_mu[l], (0, RWKV_PAD - RWKV_COLS))
        y_rwkv = rwkv_mixer(u_rwkv, mu, rwkv_w0[l], rwkv_w2[l], rwkv_a0[l], rwkv_a2[l], g2,
                            rwkv_k_k[l], rwkv_k_a[l], rwkv_r_k[l].reshape(-1), rwkv_ln_w[l], rwkv_ln_b[l])
        y_ssm = mamba2_mixer(u_ssm, ssm_conv_w[l], ssm_conv_b[l], ssm_dt_bias[l], ssm_a_log[l],
                             ssm_d[l], ssm_norm_g[l])

        merged = merge_proj(y_att, y_rwkv, y_ssm, p_attn_b, p_rwkv_b, p_ssm_b, gates, TM, TN, l)
        xt = matmul_residual(merged, w_out_b, xt, TM, TN, "w_out", l)

        h2 = rms_norm(xt, norm_ffn_g[l], BF16)
        act = ffn_up_ws(h2, w_ffn_gate, w_ffn_up, l, 2 * TM, 256)
        xt = matmul_residual(act, w_down_b, xt, TM // 2, TN, "ffn_down", l)
    out = rms_norm(xt, norm_final_g, F32)
    return out.reshape(B, T, D)
```

```python
import functools

import jax
import jax.numpy as jnp
from jax import lax
from jax.experimental import pallas as pl
from jax.experimental.pallas import tpu as pltpu

F32 = jnp.float32
BF16 = jnp.bfloat16

D_MODEL = 4096
DEPTH = 2
NORM_EPS = 1e-6
ATT_GROUPS = ((128, 1), (512, 4), (2048, 16))
ATT_HEADS = 8
ATT_DIM = 128
ATT_WIDTH = 3072
ATT_OUT = 1024
ATT_COLS = 9216
RWKV_WIDTH = 3072
RWKV_HEAD = 64
RWKV_LORA_W = 128
RWKV_LORA_A = 128
RWKV_LORA_G = 480
RWKV_COLS = 9952
RWKV_LN_EPS = 64e-5
SSM_INNER = 3072
SSM_HEADS = 48
SSM_GROUPS = 8
SSM_STATE = 128
SSM_CONV = 4
SSM_CHUNK = 128
SSM_CONV_DIM = 5120
SSM_COLS = 8240
SSM_NORM_EPS = 1e-5
FFN_HIDDEN = 11008
GATE_COLS = 3 * D_MODEL

RWKV_PAD = 10240
RWKV_G_PAD = 512
SSM_PAD = 8704

LANES = 128
RWKV_CHUNK = 64
VMEM_LIMIT = 56 * 1024 * 1024
FFN_UP_VMEM_LIMIT = 61 * 1024 * 1024


def _params(sem, vmem_limit=VMEM_LIMIT):
    return pltpu.CompilerParams(dimension_semantics=sem, vmem_limit_bytes=vmem_limit)


def _bdot(a, b):
    return jnp.dot(a.astype(BF16), b.astype(BF16), preferred_element_type=F32)


def _bdot_nt(a, b):
    return lax.dot_general(a.astype(BF16), b.astype(BF16), (((1,), (1,)), ((), ())),
                           preferred_element_type=F32)


def _bdot_tn(a, b):
    return lax.dot_general(a.astype(BF16), b.astype(BF16), (((0,), (0,)), ((), ())),
                           preferred_element_type=F32)


def _split2(x):
    hi = x.astype(BF16)
    lo = (x - hi.astype(F32)).astype(BF16)
    return hi, lo


def _split3(x):
    hi = x.astype(BF16)
    r1 = x - hi.astype(F32)
    mid = r1.astype(BF16)
    lo = (r1 - mid.astype(F32)).astype(BF16)
    return hi, mid, lo


def _dot_exact_lhs(m_bf16, x):
    hi, mid, lo = _split3(x)
    d = functools.partial(jnp.dot, preferred_element_type=F32)
    return d(m_bf16, hi) + d(m_bf16, mid) + d(m_bf16, lo)


NEG_LOG2E = -1.4426950408889634


def _sigmoid(x):
    return 1.0 / (1.0 + jnp.exp2(x * NEG_LOG2E))


def _silu(x):
    return x * _sigmoid(x)


def _softplus(x):
    return jnp.maximum(x, 0.0) + jnp.log(1.0 + jnp.exp2(jnp.abs(x) * NEG_LOG2E))


def _rms_kernel(x_ref, g_ref, o_ref):
    x = x_ref[...]
    ms = jnp.mean(x * x, axis=-1, keepdims=True)
    o_ref[...] = (x * lax.rsqrt(ms + NORM_EPS) * g_ref[...]).astype(o_ref.dtype)


def rms_norm(x, g, out_dtype, tr=512):
    T, D = x.shape
    return pl.pallas_call(
        _rms_kernel,
        out_shape=jax.ShapeDtypeStruct((T, D), out_dtype),
        grid=(T // tr,),
        in_specs=[pl.BlockSpec((tr, D), lambda i: (i, 0)),
                  pl.BlockSpec((1, D), lambda i: (0, 0))],
        out_specs=pl.BlockSpec((tr, D), lambda i: (i, 0)),
        compiler_params=_params(("parallel",)),
        name="rms_norm",
    )(x, g.reshape(1, D))


def _mm_nt_kernel(a_ref, w_ref, o_ref):
    o_ref[...] = lax.dot_general(a_ref[...], w_ref[0], (((1,), (1,)), ((), ())),
                                 preferred_element_type=F32).astype(o_ref.dtype)


def matmul_nt(a, wt, layer, row0, n_out, out_dtype, tm, tn, name):
    M, K = a.shape
    assert row0 % 16 == 0
    return pl.pallas_call(
        _mm_nt_kernel,
        out_shape=jax.ShapeDtypeStruct((M, n_out), out_dtype),
        grid=(M // tm, n_out // tn),
        in_specs=[pl.BlockSpec((tm, K), lambda i, j: (i, 0)),
                  pl.BlockSpec((pl.Element(1), pl.Element(tn), pl.Element(K)),
                               lambda i, j: (layer, pl.multiple_of(row0 + j * tn, 16), 0))],
        out_specs=pl.BlockSpec((tm, tn), lambda i, j: (i, j)),
        compiler_params=_params(("parallel", "arbitrary")),
        name=name,
    )(a, wt)


def _w_spec(w, layer, tn, col):
    if w.ndim == 2:
        return pl.BlockSpec((w.shape[0], tn), lambda i, j: (0, col(j)))
    return pl.BlockSpec((None, w.shape[1], tn), lambda i, j: (layer, 0, col(j)))


def _ffn_up_ws_kernel(h_ref, wg_ref, wu_ref, o_ref, wgb_ref, wub_ref):
    @pl.when(pl.program_id(1) == 0)
    def _():
        wgb_ref[...] = wg_ref[...].astype(BF16)
        wub_ref[...] = wu_ref[...].astype(BF16)

    h = h_ref[...]
    gate = jnp.dot(h, wgb_ref[...], preferred_element_type=F32)
    up = jnp.dot(h, wub_ref[...], preferred_element_type=F32)
    o_ref[...] = (_silu(gate) * up).astype(o_ref.dtype)


def ffn_up_ws(h, wg_stack, wu_stack, layer, tm, tn):
    M, K = h.shape
    N = wg_stack.shape[2]
    wspec = pl.BlockSpec((None, K, tn), lambda j, i: (layer, 0, j))
    return pl.pallas_call(
        _ffn_up_ws_kernel,
        out_shape=jax.ShapeDtypeStruct((M, N), BF16),
        grid=(N // tn, M // tm),
        in_specs=[pl.BlockSpec((tm, K), lambda j, i: (i, 0)), wspec, wspec],
        out_specs=pl.BlockSpec((tm, tn), lambda j, i: (i, j)),
        scratch_shapes=[pltpu.VMEM((K, tn), BF16)] * 2,
        compiler_params=_params(("arbitrary", "arbitrary"), FFN_UP_VMEM_LIMIT),
        name="ffn_up",
    )(h, wg_stack, wu_stack)


def _mm_res_kernel(a_ref, w_ref, x_ref, o_ref):
    o_ref[...] = x_ref[...] + jnp.dot(a_ref[...], w_ref[...], preferred_element_type=F32)


def matmul_residual(a, w, x, tm, tn, name, layer=0):
    M, K = a.shape
    N = w.shape[-1]
    return pl.pallas_call(
        _mm_res_kernel,
        out_shape=jax.ShapeDtypeStruct((M, N), F32),
        grid=(M // tm, N // tn),
        in_specs=[pl.BlockSpec((tm, K), lambda i, j: (i, 0)),
                  _w_spec(w, layer, tn, lambda j: j),
                  pl.BlockSpec((tm, tn), lambda i, j: (i, j))],
        out_specs=pl.BlockSpec((tm, tn), lambda i, j: (i, j)),
        compiler_params=_params(("parallel", "arbitrary")),
        name=name,
    )(a, w, x)


def _merge_proj_kernel(ya_ref, yr_ref, ys_ref, pa_ref, pr_ref, ps_ref,
                       ga_ref, gr_ref, gs_ref, o_ref):
    def sig(ref):
        return _sigmoid(ref[...].astype(F32))

    acc = sig(ga_ref) * jnp.dot(ya_ref[...], pa_ref[...], preferred_element_type=F32)
    acc += sig(gr_ref) * jnp.dot(yr_ref[...], pr_ref[...], preferred_element_type=F32)
    acc += sig(gs_ref) * jnp.dot(ys_ref[...], ps_ref[...], preferred_element_type=F32)
    o_ref[...] = acc.astype(o_ref.dtype)


def merge_proj(y_att, y_rwkv, y_ssm, p_att, p_rwkv, p_ssm, gates, tm, tn, layer):
    M = y_att.shape[0]
    N = p_att.shape[-1]
    nb = N // tn
    return pl.pallas_call(
        _merge_proj_kernel,
        out_shape=jax.ShapeDtypeStruct((M, N), BF16),
        grid=(M // tm, nb),
        in_specs=[pl.BlockSpec((tm, y_att.shape[1]), lambda i, j: (i, 0)),
                  pl.BlockSpec((tm, y_rwkv.shape[1]), lambda i, j: (i, 0)),
                  pl.BlockSpec((tm, y_ssm.shape[1]), lambda i, j: (i, 0)),
                  _w_spec(p_att, layer, tn, lambda j: j),
                  _w_spec(p_rwkv, layer, tn, lambda j: j),
                  _w_spec(p_ssm, layer, tn, lambda j: j),
                  pl.BlockSpec((tm, tn), lambda i, j: (i, j)),
                  pl.BlockSpec((tm, tn), lambda i, j: (i, j + nb)),
                  pl.BlockSpec((tm, tn), lambda i, j: (i, j + 2 * nb))],
        out_specs=pl.BlockSpec((tm, tn), lambda i, j: (i, j)),
        compiler_params=_params(("parallel", "arbitrary")),
        name="merge_proj",
    )(y_att, y_rwkv, y_ssm, p_att, p_rwkv, p_ssm, gates, gates, gates)


ATT_WIN = 2048


def _rows(start, size, stride):
    return pl.ds(start, size) if stride == 1 else pl.ds(start, size, stride=stride)


def _attn_kernel(*refs):
    n_g = len(ATT_GROUPS)
    ins = refs[:5 * n_g]
    y_ref = refs[5 * n_g]
    scr = refs[5 * n_g + 1:]
    kext, vext = scr[0:n_g], scr[n_g:2 * n_g]
    o_scr, l_scr = scr[2 * n_g], scr[2 * n_g + 1]
    w = pl.program_id(1)
    blk = ATT_DIM
    qi = lax.broadcasted_iota(jnp.int32, (blk, 2 * blk), 0)
    kj = lax.broadcasted_iota(jnp.int32, (blk, 2 * blk), 1)
    dist = qi + blk - kj
    scale = ATT_DIM ** -0.5

    for g, (window, d) in enumerate(ATT_GROUPS):
        q_ref, k_ref, v_ref, kp_ref, vp_ref = ins[5 * g:5 * g + 5]
        halo = d * blk
        kext[g][0:halo, :] = kp_ref[...]
        kext[g][halo:halo + ATT_WIN, :] = k_ref[...]
        vext[g][0:halo, :] = vp_ref[...]
        vext[g][halo:halo + ATT_WIN, :] = v_ref[...]
        band = (dist >= 0) & (dist <= window // d)
        band_first = band & jnp.logical_or(w > 0, kj >= blk)
        for res in range(d):
            for m in range(ATT_WIN // halo):
                row0 = res + halo * m
                q = q_ref[_rows(row0, blk, d), :].astype(BF16)
                k = kext[g][_rows(row0, 2 * blk, d), :].astype(BF16)
                v = vext[g][_rows(row0, 2 * blk, d), :].astype(BF16)
                s = lax.dot_general(q, k, (((1,), (1,)), ((), ())), preferred_element_type=F32) * scale
                s = jnp.where(band if m > 0 else band_first, s, -jnp.inf)
                mx = jnp.max(s, axis=-1, keepdims=True)
                p = jnp.exp(s - mx)
                den = jnp.sum(p, axis=-1, keepdims=True)
                o = jnp.dot((p / den).astype(BF16), v, preferred_element_type=F32)
                o_scr[g, _rows(row0, blk, d), :] = o
                l_scr[g, _rows(row0, blk, d), :] = jnp.broadcast_to(mx + jnp.log(den), (blk, blk))

    rc = 256
    for c in range(ATT_WIN // rc):
        rows = slice(c * rc, (c + 1) * rc)
        la, lb, lc = l_scr[0, rows, :], l_scr[1, rows, :], l_scr[2, rows, :]
        mx = jnp.maximum(jnp.maximum(la, lb), lc)
        ea, eb, ec = jnp.exp(la - mx), jnp.exp(lb - mx), jnp.exp(lc - mx)
        y = (ea * o_scr[0, rows, :] + eb * o_scr[1, rows, :] + ec * o_scr[2, rows, :]) / (ea + eb + ec)
        y_ref[rows, :] = y.astype(y_ref.dtype)


def attention_mixer(qkv):
    T = qkv.shape[0]
    blk = ATT_DIM
    n_g = len(ATT_GROUPS)
    sect = n_g * ATT_HEADS
    in_specs, scratch_k, scratch_v = [], [], []
    for g, (window, d) in enumerate(ATT_GROUPS):
        per_win = ATT_WIN // (d * blk)
        for s in range(3):
            in_specs.append(pl.BlockSpec((ATT_WIN, blk),
                                         lambda h, w, s=s, g=g: (w, s * sect + g * ATT_HEADS + h)))
        for s in (1, 2):
            in_specs.append(pl.BlockSpec(
                (d * blk, blk),
                lambda h, w, s=s, g=g, per_win=per_win: (jnp.maximum(w * per_win - 1, 0),
                                                         s * sect + g * ATT_HEADS + h)))
        scratch_k.append(pltpu.VMEM((ATT_WIN + d * blk, blk), F32))
        scratch_v.append(pltpu.VMEM((ATT_WIN + d * blk, blk), F32))
    return pl.pallas_call(
        _attn_kernel,
        out_shape=jax.ShapeDtypeStruct((T, ATT_OUT), BF16),
        grid=(ATT_HEADS, T // ATT_WIN),
        in_specs=in_specs,
        out_specs=pl.BlockSpec((ATT_WIN, blk), lambda h, w: (w, h)),
        scratch_shapes=scratch_k + scratch_v + [pltpu.VMEM((n_g, ATT_WIN, blk), F32)] * 2,
        compiler_params=_params(("parallel", "arbitrary")),
        name="attention",
    )(*([qkv] * (5 * n_g)))


def _rwkv_prep_kernel(u_ref, mu_ref, w0_ref, w2_ref, a0_ref, a2_ref, g2_ref, kk_ref, ka_ref, rk_ref,
                      r_out, lw_out, k_out, v_out, kkn_out, a_out, g_out, bonus_out, carry_ref):
    i = pl.program_id(0)
    tb = u_ref.shape[0]
    W = RWKV_WIDTH
    CW = 2 * LANES

    @pl.when(i == 0)
    def _():
        carry_ref[...] = jnp.zeros_like(carry_ref)

    def mixed(lo, width):
        sl = slice(lo, lo + width)
        u = u_ref[:, sl]
        row = lax.broadcasted_iota(jnp.int32, u.shape, 0)
        u_prev = jnp.where(row == 0, jnp.broadcast_to(carry_ref[0:1, sl], u.shape), pltpu.roll(u, 1, 0))
        return u + (u_prev - u) * mu_ref[:, sl]

    def dot3(a_split, b_ref, sl):
        ah, al = a_split
        return jnp.dot(jnp.concatenate([ah, ah, al], axis=1), b_ref[:, sl], preferred_element_type=F32)

    m0 = lax.broadcasted_iota(jnp.int32, (tb, LANES), 1) < RWKV_HEAD

    def head_sums(x):
        cols = []
        for c in range(CW // LANES):
            xb = x[:, c * LANES:(c + 1) * LANES]
            s0 = jnp.sum(jnp.where(m0, xb, 0.0), axis=-1, keepdims=True)
            s1 = jnp.sum(jnp.where(m0, 0.0, xb), axis=-1, keepdims=True)
            cols.append(jnp.where(m0, s0, s1))
        return jnp.concatenate(cols, axis=1)

    o = 3 * W
    th = _split2(jnp.tanh(mixed(o, RWKV_LORA_W)))
    xa = _split2(mixed(o + RWKV_LORA_W, RWKV_LORA_A))
    sg = _split2(_sigmoid(mixed(o + RWKV_LORA_W + RWKV_LORA_A, RWKV_G_PAD)))
    for cc in range(W // CW):
        sl = slice(cc * CW, (cc + 1) * CW)
        r = mixed(cc * CW, CW)
        k = mixed(W + cc * CW, CW)
        v = mixed(2 * W + cc * CW, CW)
        w_log = -_softplus(-(w0_ref[:, sl] + dot3(th, w2_ref, sl))) - 0.5
        lw_out[:, sl] = -jnp.exp(w_log)
        a = _sigmoid(a0_ref[:, sl] + dot3(xa, a2_ref, sl))
        g_out[:, sl] = dot3(sg, g2_ref, sl).astype(g_out.dtype)
        kk = k * kk_ref[:, sl]
        kk = kk / jnp.maximum(jnp.sqrt(head_sums(kk * kk)), 1e-12)
        k2 = k * (1.0 + (a - 1.0) * ka_ref[:, sl])
        bonus_out[:, sl] = (head_sums(r * k2 * rk_ref[:, sl]) * v).astype(bonus_out.dtype)
        r_out[:, sl] = r.astype(r_out.dtype)
        k_out[:, sl] = k2.astype(k_out.dtype)
        v_out[:, sl] = v.astype(v_out.dtype)
        kkn_out[:, sl] = kk.astype(kkn_out.dtype)
        a_out[:, sl] = a.astype(a_out.dtype)
    carry_ref[0:1, :] = u_ref[tb - 1:tb, :]


def _hi_lo_hi(w):
    hi, lo = _split2(w)
    return jnp.concatenate([hi, lo, hi], axis=0)


def rwkv_prep(u, mu, w0, w2, a0, a2, g2, k_k, k_a, r_k, tb=128):
    T = u.shape[0]
    W = RWKV_WIDTH
    w2c, a2c, g2c = _hi_lo_hi(w2), _hi_lo_hi(a2), _hi_lo_hi(g2)
    row = lambda n: pl.BlockSpec((1, n), lambda i: (0, 0))
    full = lambda a: pl.BlockSpec(a.shape, lambda i: (0, 0))
    out_spec = pl.BlockSpec((tb, W), lambda i: (i, 0))
    return pl.pallas_call(
        _rwkv_prep_kernel,
        out_shape=[jax.ShapeDtypeStruct((T, W), F32 if i == 1 else BF16) for i in range(8)],
        grid=(T // tb,),
        in_specs=[pl.BlockSpec((tb, RWKV_PAD), lambda i: (i, 0)), row(RWKV_PAD), row(W), full(w2c),
                  row(W), full(a2c), full(g2c), row(W), row(W), row(W)],
        out_specs=[out_spec] * 8,
        scratch_shapes=[pltpu.VMEM((8, RWKV_PAD), F32)],
        compiler_params=_params(("arbitrary",)),
        name="rwkv_prep",
    )(u, mu.reshape(1, -1), w0.reshape(1, W), w2c, a0.reshape(1, W), a2c, g2c,
      k_k.reshape(1, W), k_a.reshape(1, W), r_k.reshape(1, W))


def _rwkv_chunk(ins, states, consts):
    tril_l, m0, strict, incl, eye = consts
    P = range(len(ins))
    L = ins[0][0].shape[0]
    n2 = 2 * L

    def stack(x):
        return jnp.concatenate([jnp.where(m0, x, 0.0), jnp.where(m0, 0.0, x)], axis=0)

    c = [_dot_exact_lhs(tril_l, ins[p][1]) for p in P]
    c_last = [c[p][L - 1:L, :] for p in P]
    lhs, rhs, r_s, a_s, v_s, lhs2 = [], [], [], [], [], []
    for p in P:
        r, lw, k, v, kk, a = ins[p]
        n_in = jnp.exp(-c[p])
        to_end = jnp.exp(c_last[p] - c[p])
        a_t = -kk * jnp.exp(c[p] - lw)
        b_raw = kk * a
        b_t = b_raw * n_in
        k_t = k * n_in
        a_s.append(stack(a_t))
        r_s.append(stack(r * jnp.exp(c[p])))
        v_s.append(stack(v))
        lhs.append(jnp.concatenate([a_s[p], r_s[p]], axis=0))
        rhs.append(jnp.concatenate([b_t, b_t, k_t, k_t], axis=0))
        lhs2.append(jnp.concatenate([stack(b_raw * to_end), stack(k * to_end)], axis=0))
    big = [_bdot_nt(lhs[p], rhs[p]) for p in P]
    a_ab = [jnp.where(strict, big[p][0:n2, 0:n2], 0.0) for p in P]
    a_ak = [jnp.where(strict, big[p][0:n2, n2:2 * n2], 0.0) for p in P]
    m_rb = [jnp.where(incl, big[p][n2:2 * n2, 0:n2], 0.0) for p in P]
    m_rk = [jnp.where(incl, big[p][n2:2 * n2, n2:2 * n2], 0.0) for p in P]

    t_inv = [eye + a_ab[p] for p in P]
    pw = [_bdot(a_ab[p], a_ab[p]) for p in P]
    av = [_bdot(a_ak[p], v_s[p]) for p in P]
    steps = (L - 1).bit_length() - 1
    for s in range(steps):
        if s < steps - 1:
            res = [_bdot(jnp.concatenate([t_inv[p], pw[p]], axis=0), pw[p]) for p in P]
            t_inv = [t_inv[p] + res[p][0:n2] for p in P]
            pw = [res[p][n2:2 * n2] for p in P]
        else:
            res = [_bdot(t_inv[p], pw[p]) for p in P]
            t_inv = [t_inv[p] + res[p] for p in P]

    sol = [_bdot(t_inv[p], jnp.concatenate([a_s[p], av[p]], axis=1)) for p in P]
    qy = [_bdot(m_rb[p], sol[p]) for p in P]
    yv = [_bdot(m_rk[p], v_s[p]) for p in P]
    gh = [_bdot_tn(lhs2[p], jnp.concatenate(
        [sol[p], jnp.concatenate([jnp.zeros_like(v_s[p]), v_s[p]], axis=1)], axis=0)) for p in P]
    g_mm = [gh[p][:, 0:LANES] for p in P]
    h_mm = [gh[p][:, LANES:2 * LANES] for p in P]
    outs = []
    for p in P:
        q_s = r_s[p] + qy[p][:, 0:LANES]
        y_s = qy[p][:, LANES:2 * LANES] + yv[p]
        q_h = q_s[0:L] + q_s[L:n2]
        y_in = y_s[0:L] + y_s[L:n2]
        g_mat = g_mm[p] + jnp.where(
            eye > 0, jnp.broadcast_to(jnp.exp(c_last[p]), (LANES, LANES)), 0.0)
        y = _bdot(q_h, states[p]) + y_in
        s_new = _bdot(g_mat, states[p]) + h_mm[p]
        outs.append((y, s_new))
    return outs


def _rwkv_scan_kernel(r_ref, lw_ref, k_ref, v_ref, kk_ref, a_ref, g_ref, bonus_ref, lnw_ref, lnb_ref,
                      y_ref, s_ref, yp_ref, *, pairs):
    c = pl.program_id(1)
    L = r_ref.shape[0]
    n2 = 2 * L

    @pl.when(c == 0)
    def _():
        s_ref[...] = jnp.zeros_like(s_ref)
        yp_ref[...] = jnp.zeros_like(yp_ref)

    ti = lax.broadcasted_iota(jnp.int32, (L, L), 0)
    tj = lax.broadcasted_iota(jnp.int32, (L, L), 1)
    tril_l = jnp.where(ti >= tj, 1.0, 0.0).astype(BF16)
    m0 = lax.broadcasted_iota(jnp.int32, (L, LANES), 1) < RWKV_HEAD
    si = lax.broadcasted_iota(jnp.int32, (n2, n2), 0)
    sj = lax.broadcasted_iota(jnp.int32, (n2, n2), 1)
    same = (si // L) == (sj // L)
    strict = same & ((sj % L) < (si % L))
    incl = same & ((sj % L) <= (si % L))
    eye = jnp.where(si == sj, 1.0, 0.0).astype(F32)
    consts = (tril_l, m0, strict, incl, eye)

    def half_sums(x):
        s0 = jnp.sum(jnp.where(m0, x, 0.0), axis=-1, keepdims=True)
        s1 = jnp.sum(jnp.where(m0, 0.0, x), axis=-1, keepdims=True)
        return jnp.where(m0, s0, s1)

    sls = [slice(p * LANES, (p + 1) * LANES) for p in range(pairs)]
    ins = [tuple(ref[:, sl].astype(F32) for ref in (r_ref, lw_ref, k_ref, v_ref, kk_ref, a_ref))
           for sl in sls]
    outs = _rwkv_chunk(ins, [s_ref[p] for p in range(pairs)], consts)
    cur = c % 2
    for p in range(pairs):
        yp_ref[cur, p] = outs[p][0]
        s_ref[p] = outs[p][1]

    inv_n = 1.0 / RWKV_HEAD
    for p in range(pairs):
        sl = sls[p]
        y = yp_ref[1 - cur, p]
        yc = y - half_sums(y) * inv_n
        var = half_sums(yc * yc) * inv_n
        out = yc * lax.rsqrt(var + RWKV_LN_EPS) * lnw_ref[:, sl] + lnb_ref[:, sl] + bonus_ref[:, sl].astype(F32)
        y_ref[:, sl] = (out * g_ref[:, sl].astype(F32)).astype(y_ref.dtype)


def rwkv_scan(r, lw, k, v, kk, a, g, bonus, ln_w, ln_b, pairs=24):
    T, W = r.shape
    L = RWKV_CHUNK
    wb = pairs * LANES
    nc = T // L
    spec = pl.BlockSpec((L, wb), lambda p, c: (jnp.minimum(c, nc - 1), p))
    late = pl.BlockSpec((L, wb), lambda p, c: (jnp.maximum(c - 1, 0), p))
    row = pl.BlockSpec((1, wb), lambda p, c: (0, p))
    return pl.pallas_call(
        functools.partial(_rwkv_scan_kernel, pairs=pairs),
        out_shape=jax.ShapeDtypeStruct((T, W), BF16),
        grid=(W // wb, nc + 1),
        in_specs=[spec] * 6 + [late, late, row, row],
        out_specs=late,
        scratch_shapes=[pltpu.VMEM((pairs, LANES, LANES), F32), pltpu.VMEM((2, pairs, L, LANES), F32)],
        compiler_params=_params(("parallel", "arbitrary")),
        name="rwkv_scan",
    )(r, lw, k, v, kk, a, g, bonus, ln_w.reshape(1, W), ln_b.reshape(1, W))


def rwkv_mixer(u, mu, w0, w2, a0, a2, g2, k_k, k_a, r_k, ln_w, ln_b):
    r, lw, k, v, kk, a, g, bonus = rwkv_prep(u, mu, w0, w2, a0, a2, g2, k_k, k_a, r_k)
    return rwkv_scan(r, lw, k, v, kk, a, g, bonus, ln_w, ln_b)


def _ssd_chunk(c, p_ref, cw_ref, cb_ref, dtb_ref, alog_ref, dsk_ref, ng_ref, o_ref,
               ext_ref, xbc_ref, st_ref, y_scr):
    L = p_ref.shape[0]
    X0 = SSM_INNER
    HP = 2 * 64

    @pl.when(c == 0)
    def _():
        ext_ref[0:8, :] = jnp.zeros((8, SSM_CONV_DIM), F32)
        st_ref[...] = jnp.zeros_like(st_ref)

    ext_ref[8:8 + L, :] = p_ref[:, X0:X0 + SSM_CONV_DIM]
    CW = 4 * LANES
    for cc in range(SSM_CONV_DIM // CW):
        sl = slice(cc * CW, (cc + 1) * CW)
        conv = jnp.broadcast_to(cb_ref[:, sl], (L, CW))
        for kk in range(SSM_CONV):
            conv = conv + cw_ref[kk:kk + 1, sl] * ext_ref[pl.ds(8 - (SSM_CONV - 1) + kk, L), sl]
        xbc_ref[:, sl] = _silu(conv)
    ext_ref[0:8, :] = p_ref[L - 8:L, X0:X0 + SSM_CONV_DIM]

    dt = _softplus(p_ref[:, X0 + SSM_CONV_DIM:X0 + SSM_CONV_DIM + LANES] + dtb_ref[...])
    d_a = dt * (-jnp.exp(alog_ref[...]))
    ti = lax.broadcasted_iota(jnp.int32, (L, L), 0)
    tj = lax.broadcasted_iota(jnp.int32, (L, L), 1)
    causal = ti >= tj
    a_cum = _dot_exact_lhs(jnp.where(causal, 1.0, 0.0).astype(BF16), d_a)
    a_cum_t = a_cum.T
    dt_t = dt.T
    first = lax.broadcasted_iota(jnp.int32, (L, HP), 1) < 64
    first_n = lax.broadcasted_iota(jnp.int32, (SSM_STATE, HP), 1) < 64

    for g in range(SSM_GROUPS):
        b_m = xbc_ref[:,SSM_INNER + g * SSM_STATE:SSM_INNER + (g + 1) * SSM_STATE]
        c_m = xbc_ref[:,SSM_INNER + SSM_GROUPS * SSM_STATE + g * SSM_STATE:
                  SSM_INNER + SSM_GROUPS * SSM_STATE + (g + 1) * SSM_STATE]
        cb = _bdot_nt(c_m, b_m)
        b_t = b_m.T
        for j in range(3):
            pair = g * 3 + j
            xs = xbc_ref[:,pair * HP:(pair + 1) * HP]
            yd, st, ea, cd = [], [], [], []
            for hh in (2 * pair, 2 * pair + 1):
                col = a_cum[:, hh:hh + 1]
                row = a_cum_t[hh:hh + 1, :]
                dtr = dt_t[hh:hh + 1, :]
                last = a_cum[L - 1:L, hh:hh + 1]
                lmat = jnp.exp(jnp.where(causal, col - row, -jnp.inf))
                yd.append(_bdot(cb * lmat * dtr, xs))
                st.append(_bdot(b_t * (jnp.exp(last - row) * dtr), xs))
                ea.append(jnp.exp(col))
                cd.append(jnp.exp(last))
            s_in = st_ref[pair]
            y = jnp.where(first, yd[0], yd[1])
            y = y + _bdot(c_m, s_in) * jnp.where(first, ea[0], ea[1])
            st_ref[pair] = s_in * jnp.where(first_n, cd[0], cd[1]) + jnp.where(first_n, st[0], st[1])
            y_scr[:, pair * HP:(pair + 1) * HP] = y + xs * dsk_ref[:, pair * HP:(pair + 1) * HP]

    gw = SSM_INNER // SSM_GROUPS
    for g in range(SSM_GROUPS):
        sl = slice(g * gw, (g + 1) * gw)
        yg = y_scr[:, sl] * _silu(p_ref[:, sl])
        ms = jnp.mean(yg * yg, axis=-1, keepdims=True)
        o_ref[:, sl] = (yg * lax.rsqrt(ms + SSM_NORM_EPS) * ng_ref[:, sl]).astype(o_ref.dtype)


def _ssd_kernel(*refs):
    _ssd_chunk(pl.program_id(0), *refs)


def mamba2_mixer(u, conv_w, conv_b, dt_bias, a_log, d_skip, norm_g):
    T = u.shape[0]
    L = SSM_CHUNK
    padh = lambda x: jnp.pad(x, (0, LANES - SSM_HEADS)).reshape(1, LANES)
    full = lambda a: pl.BlockSpec(a.shape, lambda i: (0,) * a.ndim)
    args = (u, conv_w, conv_b.reshape(1, -1), padh(dt_bias), padh(a_log),
            jnp.repeat(d_skip, 64).reshape(1, SSM_INNER), norm_g.reshape(1, SSM_INNER))
    return pl.pallas_call(
        _ssd_kernel,
        out_shape=jax.ShapeDtypeStruct((T, SSM_INNER), BF16),
        grid=(T // L,),
        in_specs=[pl.BlockSpec((L, SSM_PAD), lambda i: (i, 0))] + [full(a) for a in args[1:]],
        out_specs=pl.BlockSpec((L, SSM_INNER), lambda i: (i, 0)),
        scratch_shapes=[pltpu.VMEM((L + 8, SSM_CONV_DIM), F32),
                        pltpu.VMEM((L, SSM_CONV_DIM), F32),
                        pltpu.VMEM((SSM_HEADS // 2, SSM_STATE, LANES), F32),
                        pltpu.VMEM((L, SSM_INNER), F32)],
        compiler_params=_params(("arbitrary",)),
        name="ssd",
    )(*args)


def kernel(x, norm_mix_g, w_in, rwkv_mu, rwkv_w0, rwkv_w2, rwkv_a0, rwkv_a2, rwkv_g2, rwkv_k_k, rwkv_k_a, rwkv_r_k, rwkv_ln_w, rwkv_ln_b, ssm_conv_w, ssm_conv_b, ssm_dt_bias, ssm_a_log, ssm_d, ssm_norm_g, p_attn, p_rwkv, p_ssm, w_out, norm_ffn_g, w_ffn_gate, w_ffn_up, w_ffn_down, norm_final_g):
    B, T, D = x.shape
    assert B == 1 and D == D_MODEL
    xt = x.reshape(T, D)
    o_r = ATT_COLS
    o_s = ATT_COLS + RWKV_COLS
    o_g = ATT_COLS + RWKV_COLS + SSM_COLS
    TM, TN = 1024, 512
    p_attn_b, p_rwkv_b, p_ssm_b = p_attn.astype(BF16), p_rwkv.astype(BF16), p_ssm.astype(BF16)
    w_out_b, w_down_b = w_out.astype(BF16), w_ffn_down.astype(BF16)
    wt_in = jnp.swapaxes(w_in, 1, 2).astype(BF16)
    for l in range(DEPTH):
        h = rms_norm(xt, norm_mix_g[l], BF16)
        qkv = matmul_nt(h, wt_in, l, 0, ATT_COLS, F32, TM, 2 * TN, "proj_att")
        u_rwkv = matmul_nt(h, wt_in, l, o_r, RWKV_PAD, F32, TM, 2 * TN, "proj_rwkv")
        u_ssm = matmul_nt(h, wt_in, l, o_s, SSM_PAD, F32, 2 * TM, TN, "proj_ssm")
        gates = matmul_nt(h, wt_in, l, o_g, GATE_COLS, BF16, TM, 2 * TN, "proj_gate")

        y_att = attention_mixer(qkv)
        g2 = jnp.pad(rwkv_g2[l], ((0, RWKV_G_PAD - RWKV_LORA_G), (0, 0)))
        mu = jnp.pad(rwkv_mu[l], (0, RWKV_PAD - RWKV_COLS))
        y_rwkv = rwkv_mixer(u_rwkv, mu, rwkv_w0[l], rwkv_w2[l], rwkv_a0[l], rwkv_a2[l], g2,
                            rwkv_k_k[l], rwkv_k_a[l], rwkv_r_k[l].reshape(-1), rwkv_ln_w[l], rwkv_ln_b[l])
        y_ssm = mamba2_mixer(u_ssm, ssm_conv_w[l], ssm_conv_b[l], ssm_dt_bias[l], ssm_a_log[l],
                             ssm_d[l], ssm_norm_g[l])

        merged = merge_proj(y_att, y_rwkv, y_ssm, p_attn_b, p_rwkv_b, p_ssm_b, gates, TM, TN, l)
        xt = matmul_residual(merged, w_out_b, xt, TM, 2 * TN, "w_out", l)

        h2 = rms_norm(xt, norm_ffn_g[l], BF16)
        act = ffn_up_ws(h2, w_ffn_gate, w_ffn_up, l, 2 * TM, 256)
        xt = matmul_residual(act, w_down_b, xt, TM // 2, TN, "ffn_down", l)
    out = rms_norm(xt, norm_final_g, F32)
    return out.reshape(B, T, D)
```

```python
import functools

import jax
import jax.numpy as jnp
from jax import lax
from jax.experimental import pallas as pl
from jax.experimental.pallas import tpu as pltpu

F32 = jnp.float32
BF16 = jnp.bfloat16

D_MODEL = 4096
DEPTH = 2
NORM_EPS = 1e-6
ATT_GROUPS = ((128, 1), (512, 4), (2048, 16))
ATT_HEADS = 8
ATT_DIM = 128
ATT_WIDTH = 3072
ATT_OUT = 1024
ATT_COLS = 9216
RWKV_WIDTH = 3072
RWKV_HEAD = 64
RWKV_LORA_W = 128
RWKV_LORA_A = 128
RWKV_LORA_G = 480
RWKV_COLS = 9952
RWKV_LN_EPS = 64e-5
SSM_INNER = 3072
SSM_HEADS = 48
SSM_GROUPS = 8
SSM_STATE = 128
SSM_CONV = 4
SSM_CHUNK = 128
SSM_CONV_DIM = 5120
SSM_COLS = 8240
SSM_NORM_EPS = 1e-5
FFN_HIDDEN = 11008
GATE_COLS = 3 * D_MODEL

RWKV_PAD = 10240
RWKV_G_PAD = 512
SSM_PAD = 8704

LANES = 128
RWKV_CHUNK = 64
VMEM_LIMIT = 56 * 1024 * 1024
FFN_UP_VMEM_LIMIT = 61 * 1024 * 1024


def _params(sem, vmem_limit=VMEM_LIMIT):
    return pltpu.CompilerParams(dimension_semantics=sem, vmem_limit_bytes=vmem_limit)


def _bdot(a, b):
    return jnp.dot(a.astype(BF16), b.astype(BF16), preferred_element_type=F32)


def _bdot_nt(a, b):
    return lax.dot_general(a.astype(BF16), b.astype(BF16), (((1,), (1,)), ((), ())),
                           preferred_element_type=F32)


def _bdot_tn(a, b):
    return lax.dot_general(a.astype(BF16), b.astype(BF16), (((0,), (0,)), ((), ())),
                           preferred_element_type=F32)


def _split2(x):
    hi = x.astype(BF16)
    lo = (x - hi.astype(F32)).astype(BF16)
    return hi, lo


def _split3(x):
    hi = x.astype(BF16)
    r1 = x - hi.astype(F32)
    mid = r1.astype(BF16)
    lo = (r1 - mid.astype(F32)).astype(BF16)
    return hi, mid, lo


def _dot_exact_lhs(m_bf16, x):
    hi, mid, lo = _split3(x)
    d = functools.partial(jnp.dot, preferred_element_type=F32)
    return d(m_bf16, hi) + d(m_bf16, mid) + d(m_bf16, lo)


NEG_LOG2E = -1.4426950408889634


def _sigmoid(x):
    return 1.0 / (1.0 + jnp.exp2(x * NEG_LOG2E))


def _silu(x):
    return x * _sigmoid(x)


def _softplus(x):
    return jnp.maximum(x, 0.0) + jnp.log(1.0 + jnp.exp2(jnp.abs(x) * NEG_LOG2E))


def _rms_kernel(x_ref, g_ref, o_ref):
    x = x_ref[...]
    ms = jnp.mean(x * x, axis=-1, keepdims=True)
    o_ref[...] = (x * lax.rsqrt(ms + NORM_EPS) * g_ref[...]).astype(o_ref.dtype)


def rms_norm(x, g, out_dtype, tr=512):
    T, D = x.shape
    return pl.pallas_call(
        _rms_kernel,
        out_shape=jax.ShapeDtypeStruct((T, D), out_dtype),
        grid=(T // tr,),
        in_specs=[pl.BlockSpec((tr, D), lambda i: (i, 0)),
                  pl.BlockSpec((1, D), lambda i: (0, 0))],
        out_specs=pl.BlockSpec((tr, D), lambda i: (i, 0)),
        compiler_params=_params(("parallel",)),
        name="rms_norm",
    )(x, g.reshape(1, D))


def _mm_nt_kernel(a_ref, w_ref, o_ref):
    o_ref[...] = lax.dot_general(a_ref[...], w_ref[...], (((1,), (1,)), ((), ())),
                                 preferred_element_type=F32).astype(o_ref.dtype)


def matmul_nt(a, wt, layer, row0, n_out, out_dtype, tm, tn, name):
    M, K = a.shape
    assert row0 % 16 == 0

    def body(a_hbm, w_hbm, o_hbm):
        pltpu.emit_pipeline(
            _mm_nt_kernel,
            grid=(M // tm, n_out // tn),
            in_specs=[pl.BlockSpec((tm, K), lambda i, j: (i, 0)),
                      pl.BlockSpec((tn, K), lambda i, j: (j, 0))],
            out_specs=[pl.BlockSpec((tm, tn), lambda i, j: (i, j))],
        )(a_hbm, w_hbm.at[layer, pl.ds(row0, n_out)], o_hbm)

    return _hbm_call(body, jax.ShapeDtypeStruct((M, n_out), out_dtype), name, 2)(a, wt)


def _ffn_up_ws_kernel(h_ref, wg_ref, wu_ref, o_ref, wgb_ref, wub_ref):
    @pl.when(pl.program_id(1) == 0)
    def _():
        wgb_ref[...] = wg_ref[...].astype(BF16)
        wub_ref[...] = wu_ref[...].astype(BF16)

    h = h_ref[...]
    gate = jnp.dot(h, wgb_ref[...], preferred_element_type=F32)
    up = jnp.dot(h, wub_ref[...], preferred_element_type=F32)
    o_ref[...] = (_silu(gate) * up).astype(o_ref.dtype)


def ffn_up_ws(h, wg_stack, wu_stack, layer, tm, tn):
    M, K = h.shape
    N = wg_stack.shape[2]
    wspec = pl.BlockSpec((None, K, tn), lambda j, i: (layer, 0, j))
    return pl.pallas_call(
        _ffn_up_ws_kernel,
        out_shape=jax.ShapeDtypeStruct((M, N), BF16),
        grid=(N // tn, M // tm),
        in_specs=[pl.BlockSpec((tm, K), lambda j, i: (i, 0)), wspec, wspec],
        out_specs=pl.BlockSpec((tm, tn), lambda j, i: (i, j)),
        scratch_shapes=[pltpu.VMEM((K, tn), BF16)] * 2,
        compiler_params=_params(("arbitrary", "arbitrary"), FFN_UP_VMEM_LIMIT),
        name="ffn_up",
    )(h, wg_stack, wu_stack)


def _mm_res_kernel(a_ref, w_ref, x_ref, o_ref):
    o_ref[...] = x_ref[...] + jnp.dot(a_ref[...], w_ref[...], preferred_element_type=F32)


def _hbm_call(pipeline_body, out_shape, name, n_in):
    any_spec = pl.BlockSpec(memory_space=pl.ANY)
    return pl.pallas_call(
        pipeline_body,
        out_shape=out_shape,
        in_specs=[any_spec] * n_in,
        out_specs=any_spec,
        compiler_params=pltpu.CompilerParams(vmem_limit_bytes=VMEM_LIMIT),
        name=name,
    )


def matmul_residual(a, w, x, tm, tn, name, layer=0):
    M, K = a.shape
    N = w.shape[-1]

    def body(a_hbm, w_hbm, x_hbm, o_hbm):
        pltpu.emit_pipeline(
            _mm_res_kernel,
            grid=(M // tm, N // tn),
            in_specs=[pl.BlockSpec((tm, K), lambda i, j: (i, 0)),
                      pl.BlockSpec((K, tn), lambda i, j: (0, j)),
                      pl.BlockSpec((tm, tn), lambda i, j: (i, j))],
            out_specs=[pl.BlockSpec((tm, tn), lambda i, j: (i, j))],
        )(a_hbm, w_hbm.at[layer], x_hbm, o_hbm)

    return _hbm_call(body, jax.ShapeDtypeStruct((M, N), F32), name, 3)(a, w, x)


def _merge_proj_kernel(ya_ref, yr_ref, ys_ref, pa_ref, pr_ref, ps_ref,
                       ga_ref, gr_ref, gs_ref, o_ref):
    def sig(ref):
        return _sigmoid(ref[...].astype(F32))

    acc = sig(ga_ref) * jnp.dot(ya_ref[...], pa_ref[...], preferred_element_type=F32)
    acc += sig(gr_ref) * jnp.dot(yr_ref[...], pr_ref[...], preferred_element_type=F32)
    acc += sig(gs_ref) * jnp.dot(ys_ref[...], ps_ref[...], preferred_element_type=F32)
    o_ref[...] = acc.astype(o_ref.dtype)


def merge_proj(y_att, y_rwkv, y_ssm, p_att, p_rwkv, p_ssm, gates, tm, tn, layer):
    M = y_att.shape[0]
    N = p_att.shape[-1]
    nb = N // tn
    rows = lambda width: pl.BlockSpec((tm, width), lambda i, j: (i, 0))
    cols = lambda depth: pl.BlockSpec((depth, tn), lambda i, j: (0, j))
    gate = lambda g: pl.BlockSpec((tm, tn), lambda i, j: (i, j + g * nb))

    def body(ya, yr, ys, pa, pr, ps, gts, o_hbm):
        pltpu.emit_pipeline(
            _merge_proj_kernel,
            grid=(M // tm, nb),
            in_specs=[rows(y_att.shape[1]), rows(y_rwkv.shape[1]), rows(y_ssm.shape[1]),
                      cols(p_att.shape[1]), cols(p_rwkv.shape[1]), cols(p_ssm.shape[1]),
                      gate(0), gate(1), gate(2)],
            out_specs=[pl.BlockSpec((tm, tn), lambda i, j: (i, j))],
        )(ya, yr, ys, pa.at[layer], pr.at[layer], ps.at[layer], gts, gts, gts, o_hbm)

    return _hbm_call(body, jax.ShapeDtypeStruct((M, N), BF16), "merge_proj", 7)(
        y_att, y_rwkv, y_ssm, p_att, p_rwkv, p_ssm, gates)


ATT_WIN = 2048


def _rows(start, size, stride):
    return pl.ds(start, size) if stride == 1 else pl.ds(start, size, stride=stride)


def _attn_kernel(*refs):
    n_g = len(ATT_GROUPS)
    ins = refs[:5 * n_g]
    y_ref = refs[5 * n_g]
    scr = refs[5 * n_g + 1:]
    kext, vext = scr[0:n_g], scr[n_g:2 * n_g]
    o_scr, l_scr = scr[2 * n_g], scr[2 * n_g + 1]
    w = pl.program_id(1)
    blk = ATT_DIM
    qi = lax.broadcasted_iota(jnp.int32, (blk, 2 * blk), 0)
    kj = lax.broadcasted_iota(jnp.int32, (blk, 2 * blk), 1)
    dist = qi + blk - kj
    scale = ATT_DIM ** -0.5

    for g, (window, d) in enumerate(ATT_GROUPS):
        q_ref, k_ref, v_ref, kp_ref, vp_ref = ins[5 * g:5 * g + 5]
        halo = d * blk
        kext[g][0:halo, :] = kp_ref[...]
        kext[g][halo:halo + ATT_WIN, :] = k_ref[...]
        vext[g][0:halo, :] = vp_ref[...]
        vext[g][halo:halo + ATT_WIN, :] = v_ref[...]
        band = (dist >= 0) & (dist <= window // d)
        band_first = band & jnp.logical_or(w > 0, kj >= blk)
        for res in range(d):
            for m in range(ATT_WIN // halo):
                row0 = res + halo * m
                q = q_ref[_rows(row0, blk, d), :].astype(BF16)
                k = kext[g][_rows(row0, 2 * blk, d), :].astype(BF16)
                v = vext[g][_rows(row0, 2 * blk, d), :].astype(BF16)
                s = lax.dot_general(q, k, (((1,), (1,)), ((), ())), preferred_element_type=F32) * scale
                s = jnp.where(band if m > 0 else band_first, s, -jnp.inf)
                mx = jnp.max(s, axis=-1, keepdims=True)
                p = jnp.exp(s - mx)
                den = jnp.sum(p, axis=-1, keepdims=True)
                o = jnp.dot((p / den).astype(BF16), v, preferred_element_type=F32)
                o_scr[g, _rows(row0, blk, d), :] = o
                l_scr[g, _rows(row0, blk, d), :] = jnp.broadcast_to(mx + jnp.log(den), (blk, blk))

    rc = 256
    for c in range(ATT_WIN // rc):
        rows = slice(c * rc, (c + 1) * rc)
        la, lb, lc = l_scr[0, rows, :], l_scr[1, rows, :], l_scr[2, rows, :]
        mx = jnp.maximum(jnp.maximum(la, lb), lc)
        ea, eb, ec = jnp.exp(la - mx), jnp.exp(lb - mx), jnp.exp(lc - mx)
        y = (ea * o_scr[0, rows, :] + eb * o_scr[1, rows, :] + ec * o_scr[2, rows, :]) / (ea + eb + ec)
        y_ref[rows, :] = y.astype(y_ref.dtype)


def attention_mixer(qkv):
    T = qkv.shape[0]
    blk = ATT_DIM
    n_g = len(ATT_GROUPS)
    sect = n_g * ATT_HEADS
    in_specs, scratch_k, scratch_v = [], [], []
    for g, (window, d) in enumerate(ATT_GROUPS):
        per_win = ATT_WIN // (d * blk)
        for s in range(3):
            in_specs.append(pl.BlockSpec((ATT_WIN, blk),
                                         lambda h, w, s=s, g=g: (w, s * sect + g * ATT_HEADS + h)))
        for s in (1, 2):
            in_specs.append(pl.BlockSpec(
                (d * blk, blk),
                lambda h, w, s=s, g=g, per_win=per_win: (jnp.maximum(w * per_win - 1, 0),
                                                         s * sect + g * ATT_HEADS + h)))
        scratch_k.append(pltpu.VMEM((ATT_WIN + d * blk, blk), F32))
        scratch_v.append(pltpu.VMEM((ATT_WIN + d * blk, blk), F32))
    return pl.pallas_call(
        _attn_kernel,
        out_shape=jax.ShapeDtypeStruct((T, ATT_OUT), BF16),
        grid=(ATT_HEADS, T // ATT_WIN),
        in_specs=in_specs,
        out_specs=pl.BlockSpec((ATT_WIN, blk), lambda h, w: (w, h)),
        scratch_shapes=scratch_k + scratch_v + [pltpu.VMEM((n_g, ATT_WIN, blk), F32)] * 2,
        compiler_params=_params(("parallel", "arbitrary")),
        name="attention",
    )(*([qkv] * (5 * n_g)))


def _rwkv_prep_kernel(u_ref, mu_ref, w0_ref, w2_ref, a0_ref, a2_ref, g2_ref, kk_ref, ka_ref, rk_ref,
                      r_out, lw_out, k_out, v_out, kkn_out, a_out, g_out, bonus_out, carry_ref):
    i = pl.program_id(0)
    tb = u_ref.shape[0]
    W = RWKV_WIDTH
    CW = 2 * LANES

    @pl.when(i == 0)
    def _():
        carry_ref[...] = jnp.zeros_like(carry_ref)

    def mixed(lo, width):
        sl = slice(lo, lo + width)
        u = u_ref[:, sl]
        row = lax.broadcasted_iota(jnp.int32, u.shape, 0)
        u_prev = jnp.where(row == 0, jnp.broadcast_to(carry_ref[0:1, sl], u.shape), pltpu.roll(u, 1, 0))
        return u + (u_prev - u) * mu_ref[:, sl]

    def dot3(a_split, b_ref, sl):
        ah, al = a_split
        return jnp.dot(jnp.concatenate([ah, ah, al], axis=1), b_ref[:, sl], preferred_element_type=F32)

    m0 = lax.broadcasted_iota(jnp.int32, (tb, LANES), 1) < RWKV_HEAD

    def head_sums(x):
        cols = []
        for c in range(CW // LANES):
            xb = x[:, c * LANES:(c + 1) * LANES]
            s0 = jnp.sum(jnp.where(m0, xb, 0.0), axis=-1, keepdims=True)
            s1 = jnp.sum(jnp.where(m0, 0.0, xb), axis=-1, keepdims=True)
            cols.append(jnp.where(m0, s0, s1))
        return jnp.concatenate(cols, axis=1)

    o = 3 * W
    th = _split2(jnp.tanh(mixed(o, RWKV_LORA_W)))
    xa = _split2(mixed(o + RWKV_LORA_W, RWKV_LORA_A))
    sg = _split2(_sigmoid(mixed(o + RWKV_LORA_W + RWKV_LORA_A, RWKV_G_PAD)))
    for cc in range(W // CW):
        sl = slice(cc * CW, (cc + 1) * CW)
        r = mixed(cc * CW, CW)
        k = mixed(W + cc * CW, CW)
        v = mixed(2 * W + cc * CW, CW)
        w_log = -_softplus(-(w0_ref[:, sl] + dot3(th, w2_ref, sl))) - 0.5
        lw_out[:, sl] = -jnp.exp(w_log)
        a = _sigmoid(a0_ref[:, sl] + dot3(xa, a2_ref, sl))
        g_out[:, sl] = dot3(sg, g2_ref, sl).astype(g_out.dtype)
        kk = k * kk_ref[:, sl]
        kk = kk / jnp.maximum(jnp.sqrt(head_sums(kk * kk)), 1e-12)
        k2 = k * (1.0 + (a - 1.0) * ka_ref[:, sl])
        bonus_out[:, sl] = (head_sums(r * k2 * rk_ref[:, sl]) * v).astype(bonus_out.dtype)
        r_out[:, sl] = r.astype(r_out.dtype)
        k_out[:, sl] = k2.astype(k_out.dtype)
        v_out[:, sl] = v.astype(v_out.dtype)
        kkn_out[:, sl] = kk.astype(kkn_out.dtype)
        a_out[:, sl] = a.astype(a_out.dtype)
    carry_ref[0:1, :] = u_ref[tb - 1:tb, :]


def _hi_lo_hi(w):
    hi, lo = _split2(w)
    return jnp.concatenate([hi, lo, hi], axis=0)


def rwkv_prep(u, mu, w0, w2, a0, a2, g2, k_k, k_a, r_k, tb=128):
    T = u.shape[0]
    W = RWKV_WIDTH
    w2c, a2c, g2c = _hi_lo_hi(w2), _hi_lo_hi(a2), _hi_lo_hi(g2)
    row = lambda n: pl.BlockSpec((1, n), lambda i: (0, 0))
    full = lambda a: pl.BlockSpec(a.shape, lambda i: (0, 0))
    out_spec = pl.BlockSpec((tb, W), lambda i: (i, 0))
    return pl.pallas_call(
        _rwkv_prep_kernel,
        out_shape=[jax.ShapeDtypeStruct((T, W), F32 if i == 1 else BF16) for i in range(8)],
        grid=(T // tb,),
        in_specs=[pl.BlockSpec((tb, RWKV_PAD), lambda i: (i, 0)), row(RWKV_PAD), row(W), full(w2c),
                  row(W), full(a2c), full(g2c), row(W), row(W), row(W)],
        out_specs=[out_spec] * 8,
        scratch_shapes=[pltpu.VMEM((8, RWKV_PAD), F32)],
        compiler_params=_params(("arbitrary",)),
        name="rwkv_prep",
    )(u, mu.reshape(1, -1), w0.reshape(1, W), w2c, a0.reshape(1, W), a2c, g2c,
      k_k.reshape(1, W), k_a.reshape(1, W), r_k.reshape(1, W))


def _rwkv_chunk(ins, states, consts):
    tril_l, m0, strict, incl, eye = consts
    P = range(len(ins))
    L = ins[0][0].shape[0]
    n2 = 2 * L

    def stack(x):
        return jnp.concatenate([jnp.where(m0, x, 0.0), jnp.where(m0, 0.0, x)], axis=0)

    c = [_dot_exact_lhs(tril_l, ins[p][1]) for p in P]
    c_last = [c[p][L - 1:L, :] for p in P]
    lhs, rhs, r_s, a_s, v_s, lhs2 = [], [], [], [], [], []
    for p in P:
        r, lw, k, v, kk, a = ins[p]
        n_in = jnp.exp(-c[p])
        to_end = jnp.exp(c_last[p] - c[p])
        a_t = -kk * jnp.exp(c[p] - lw)
        b_raw = kk * a
        b_t = b_raw * n_in
        k_t = k * n_in
        a_s.append(stack(a_t))
        r_s.append(stack(r * jnp.exp(c[p])))
        v_s.append(stack(v))
        lhs.append(jnp.concatenate([a_s[p], r_s[p]], axis=0))
        rhs.append(jnp.concatenate([b_t, b_t, k_t, k_t], axis=0))
        lhs2.append(jnp.concatenate([stack(b_raw * to_end), stack(k * to_end)], axis=0))
    big = [_bdot_nt(lhs[p], rhs[p]) for p in P]
    a_ab = [jnp.where(strict, big[p][0:n2, 0:n2], 0.0) for p in P]
    a_ak = [jnp.where(strict, big[p][0:n2, n2:2 * n2], 0.0) for p in P]
    m_rb = [jnp.where(incl, big[p][n2:2 * n2, 0:n2], 0.0) for p in P]
    m_rk = [jnp.where(incl, big[p][n2:2 * n2, n2:2 * n2], 0.0) for p in P]

    t_inv = [eye + a_ab[p] for p in P]
    pw = [_bdot(a_ab[p], a_ab[p]) for p in P]
    av = [_bdot(a_ak[p], v_s[p]) for p in P]
    steps = (L - 1).bit_length() - 1
    for s in range(steps):
        if s < steps - 1:
            res = [_bdot(jnp.concatenate([t_inv[p], pw[p]], axis=0), pw[p]) for p in P]
            t_inv = [t_inv[p] + res[p][0:n2] for p in P]
            pw = [res[p][n2:2 * n2] for p in P]
        else:
            res = [_bdot(t_inv[p], pw[p]) for p in P]
            t_inv = [t_inv[p] + res[p] for p in P]

    sol = [_bdot(t_inv[p], jnp.concatenate([a_s[p], av[p]], axis=1)) for p in P]
    qy = [_bdot(m_rb[p], sol[p]) for p in P]
    yv = [_bdot(m_rk[p], v_s[p]) for p in P]
    gh = [_bdot_tn(lhs2[p], jnp.concatenate(
        [sol[p], jnp.concatenate([jnp.zeros_like(v_s[p]), v_s[p]], axis=1)], axis=0)) for p in P]
    g_mm = [gh[p][:, 0:LANES] for p in P]
    h_mm = [gh[p][:, LANES:2 * LANES] for p in P]
    outs = []
    for p in P:
        q_s = r_s[p] + qy[p][:, 0:LANES]
        y_s = qy[p][:, LANES:2 * LANES] + yv[p]
        q_h = q_s[0:L] + q_s[L:n2]
        y_in = y_s[0:L] + y_s[L:n2]
        g_mat = g_mm[p] + jnp.where(
            eye > 0, jnp.broadcast_to(jnp.exp(c_last[p]), (LANES, LANES)), 0.0)
        y = _bdot(q_h, states[p]) + y_in
        s_new = _bdot(g_mat, states[p]) + h_mm[p]
        outs.append((y, s_new))
    return outs


def _rwkv_scan_kernel(r_ref, lw_ref, k_ref, v_ref, kk_ref, a_ref, g_ref, bonus_ref, lnw_ref, lnb_ref,
                      y_ref, s_ref, yp_ref, *, pairs):
    c = pl.program_id(1)
    L = r_ref.shape[0]
    n2 = 2 * L

    @pl.when(c == 0)
    def _():
        s_ref[...] = jnp.zeros_like(s_ref)
        yp_ref[...] = jnp.zeros_like(yp_ref)

    ti = lax.broadcasted_iota(jnp.int32, (L, L), 0)
    tj = lax.broadcasted_iota(jnp.int32, (L, L), 1)
    tril_l = jnp.where(ti >= tj, 1.0, 0.0).astype(BF16)
    m0 = lax.broadcasted_iota(jnp.int32, (L, LANES), 1) < RWKV_HEAD
    si = lax.broadcasted_iota(jnp.int32, (n2, n2), 0)
    sj = lax.broadcasted_iota(jnp.int32, (n2, n2), 1)
    same = (si // L) == (sj // L)
    strict = same & ((sj % L) < (si % L))
    incl = same & ((sj % L) <= (si % L))
    eye = jnp.where(si == sj, 1.0, 0.0).astype(F32)
    consts = (tril_l, m0, strict, incl, eye)

    def half_sums(x):
        s0 = jnp.sum(jnp.where(m0, x, 0.0), axis=-1, keepdims=True)
        s1 = jnp.sum(jnp.where(m0, 0.0, x), axis=-1, keepdims=True)
        return jnp.where(m0, s0, s1)

    sls = [slice(p * LANES, (p + 1) * LANES) for p in range(pairs)]
    ins = [tuple(ref[:, sl].astype(F32) for ref in (r_ref, lw_ref, k_ref, v_ref, kk_ref, a_ref))
           for sl in sls]
    outs = _rwkv_chunk(ins, [s_ref[p] for p in range(pairs)], consts)
    cur = c % 2
    for p in range(pairs):
        yp_ref[cur, p] = outs[p][0]
        s_ref[p] = outs[p][1]

    inv_n = 1.0 / RWKV_HEAD
    for p in range(pairs):
        sl = sls[p]
        y = yp_ref[1 - cur, p]
        yc = y - half_sums(y) * inv_n
        var = half_sums(yc * yc) * inv_n
        out = yc * lax.rsqrt(var + RWKV_LN_EPS) * lnw_ref[:, sl] + lnb_ref[:, sl] + bonus_ref[:, sl].astype(F32)
        y_ref[:, sl] = (out * g_ref[:, sl].astype(F32)).astype(y_ref.dtype)


def rwkv_scan(r, lw, k, v, kk, a, g, bonus, ln_w, ln_b, pairs=24):
    T, W = r.shape
    L = RWKV_CHUNK
    wb = pairs * LANES
    nc = T // L
    spec = pl.BlockSpec((L, wb), lambda p, c: (jnp.minimum(c, nc - 1), p))
    late = pl.BlockSpec((L, wb), lambda p, c: (jnp.maximum(c - 1, 0), p))
    row = pl.BlockSpec((1, wb), lambda p, c: (0, p))
    return pl.pallas_call(
        functools.partial(_rwkv_scan_kernel, pairs=pairs),
        out_shape=jax.ShapeDtypeStruct((T, W), BF16),
        grid=(W // wb, nc + 1),
        in_specs=[spec] * 6 + [late, late, row, row],
        out_specs=late,
        scratch_shapes=[pltpu.VMEM((pairs, LANES, LANES), F32), pltpu.VMEM((2, pairs, L, LANES), F32)],
        compiler_params=_params(("parallel", "arbitrary")),
        name="rwkv_scan",
    )(r, lw, k, v, kk, a, g, bonus, ln_w.reshape(1, W), ln_b.reshape(1, W))


def rwkv_mixer(u, mu, w0, w2, a0, a2, g2, k_k, k_a, r_k, ln_w, ln_b):
    r, lw, k, v, kk, a, g, bonus = rwkv_prep(u, mu, w0, w2, a0, a2, g2, k_k, k_a, r_k)
    return rwkv_scan(r, lw, k, v, kk, a, g, bonus, ln_w, ln_b)


def _ssd_chunk(c, p_ref, cw_ref, cb_ref, dtb_ref, alog_ref, dsk_ref, ng_ref, o_ref,
               ext_ref, xbc_ref, st_ref, y_scr):
    L = p_ref.shape[0]
    X0 = SSM_INNER
    HP = 2 * 64

    @pl.when(c == 0)
    def _():
        ext_ref[0:8, :] = jnp.zeros((8, SSM_CONV_DIM), F32)
        st_ref[...] = jnp.zeros_like(st_ref)

    ext_ref[8:8 + L, :] = p_ref[:, X0:X0 + SSM_CONV_DIM]
    CW = 4 * LANES
    for cc in range(SSM_CONV_DIM // CW):
        sl = slice(cc * CW, (cc + 1) * CW)
        conv = jnp.broadcast_to(cb_ref[:, sl], (L, CW))
        for kk in range(SSM_CONV):
            conv = conv + cw_ref[kk:kk + 1, sl] * ext_ref[pl.ds(8 - (SSM_CONV - 1) + kk, L), sl]
        xbc_ref[:, sl] = _silu(conv)
    ext_ref[0:8, :] = p_ref[L - 8:L, X0:X0 + SSM_CONV_DIM]

    dt = _softplus(p_ref[:, X0 + SSM_CONV_DIM:X0 + SSM_CONV_DIM + LANES] + dtb_ref[...])
    d_a = dt * (-jnp.exp(alog_ref[...]))
    ti = lax.broadcasted_iota(jnp.int32, (L, L), 0)
    tj = lax.broadcasted_iota(jnp.int32, (L, L), 1)
    causal = ti >= tj
    a_cum = _dot_exact_lhs(jnp.where(causal, 1.0, 0.0).astype(BF16), d_a)
    a_cum_t = a_cum.T
    dt_t = dt.T
    first = lax.broadcasted_iota(jnp.int32, (L, HP), 1) < 64
    first_n = lax.broadcasted_iota(jnp.int32, (SSM_STATE, HP), 1) < 64

    for g in range(SSM_GROUPS):
        b_m = xbc_ref[:,SSM_INNER + g * SSM_STATE:SSM_INNER + (g + 1) * SSM_STATE]
        c_m = xbc_ref[:,SSM_INNER + SSM_GROUPS * SSM_STATE + g * SSM_STATE:
                  SSM_INNER + SSM_GROUPS * SSM_STATE + (g + 1) * SSM_STATE]
        cb = _bdot_nt(c_m, b_m)
        b_t = b_m.T
        for j in range(3):
            pair = g * 3 + j
            xs = xbc_ref[:,pair * HP:(pair + 1) * HP]
            yd, st, ea, cd = [], [], [], []
            for hh in (2 * pair, 2 * pair + 1):
                col = a_cum[:, hh:hh + 1]
                row = a_cum_t[hh:hh + 1, :]
                dtr = dt_t[hh:hh + 1, :]
                last = a_cum[L - 1:L, hh:hh + 1]
                lmat = jnp.exp(jnp.where(causal, col - row, -jnp.inf))
                yd.append(_bdot(cb * lmat * dtr, xs))
                st.append(_bdot(b_t * (jnp.exp(last - row) * dtr), xs))
                ea.append(jnp.exp(col))
                cd.append(jnp.exp(last))
            s_in = st_ref[pair]
            y = jnp.where(first, yd[0], yd[1])
            y = y + _bdot(c_m, s_in) * jnp.where(first, ea[0], ea[1])
            st_ref[pair] = s_in * jnp.where(first_n, cd[0], cd[1]) + jnp.where(first_n, st[0], st[1])
            y_scr[:, pair * HP:(pair + 1) * HP] = y + xs * dsk_ref[:, pair * HP:(pair + 1) * HP]

    gw = SSM_INNER // SSM_GROUPS
    for g in range(SSM_GROUPS):
        sl = slice(g * gw, (g + 1) * gw)
        yg = y_scr[:, sl] * _silu(p_ref[:, sl])
        ms = jnp.mean(yg * yg, axis=-1, keepdims=True)
        o_ref[:, sl] = (yg * lax.rsqrt(ms + SSM_NORM_EPS) * ng_ref[:, sl]).astype(o_ref.dtype)


def _ssd_kernel(*refs):
    _ssd_chunk(pl.program_id(0), *refs)


def mamba2_mixer(u, conv_w, conv_b, dt_bias, a_log, d_skip, norm_g):
    T = u.shape[0]
    L = SSM_CHUNK
    padh = lambda x: jnp.pad(x, (0, LANES - SSM_HEADS)).reshape(1, LANES)
    full = lambda a: pl.BlockSpec(a.shape, lambda i: (0,) * a.ndim)
    args = (u, conv_w, conv_b.reshape(1, -1), padh(dt_bias), padh(a_log),
            jnp.repeat(d_skip, 64).reshape(1, SSM_INNER), norm_g.reshape(1, SSM_INNER))
    return pl.pallas_call(
        _ssd_kernel,
        out_shape=jax.ShapeDtypeStruct((T, SSM_INNER), BF16),
        grid=(T // L,),
        in_specs=[pl.BlockSpec((L, SSM_PAD), lambda i: (i, 0))] + [full(a) for a in args[1:]],
        out_specs=pl.BlockSpec((L, SSM_INNER), lambda i: (i, 0)),
        scratch_shapes=[pltpu.VMEM((L + 8, SSM_CONV_DIM), F32),
                        pltpu.VMEM((L, SSM_CONV_DIM), F32),
                        pltpu.VMEM((SSM_HEADS // 2, SSM_STATE, LANES), F32),
                        pltpu.VMEM((L, SSM_INNER), F32)],
        compiler_params=_params(("arbitrary",)),
        name="ssd",
    )(*args)


def kernel(x, norm_mix_g, w_in, rwkv_mu, rwkv_w0, rwkv_w2, rwkv_a0, rwkv_a2, rwkv_g2, rwkv_k_k, rwkv_k_a, rwkv_r_k, rwkv_ln_w, rwkv_ln_b, ssm_conv_w, ssm_conv_b, ssm_dt_bias, ssm_a_log, ssm_d, ssm_norm_g, p_attn, p_rwkv, p_ssm, w_out, norm_ffn_g, w_ffn_gate, w_ffn_up, w_ffn_down, norm_final_g):
    B, T, D = x.shape
    assert B == 1 and D == D_MODEL
    xt = x.reshape(T, D)
    o_r = ATT_COLS
    o_s = ATT_COLS + RWKV_COLS
    o_g = ATT_COLS + RWKV_COLS + SSM_COLS
    TM, TN = 1024, 512
    p_attn_b, p_rwkv_b, p_ssm_b = p_attn.astype(BF16), p_rwkv.astype(BF16), p_ssm.astype(BF16)
    w_out_b, w_down_b = w_out.astype(BF16), w_ffn_down.astype(BF16)
    wt_in = jnp.swapaxes(w_in, 1, 2).astype(BF16)
    for l in range(DEPTH):
        h = rms_norm(xt, norm_mix_g[l], BF16)
        qkv = matmul_nt(h, wt_in, l, 0, ATT_COLS, F32, TM, 2 * TN, "proj_att")
        u_rwkv = matmul_nt(h, wt_in, l, o_r, RWKV_PAD, F32, TM, 2 * TN, "proj_rwkv")
        u_ssm = matmul_nt(h, wt_in, l, o_s, SSM_PAD, F32, 2 * TM, TN, "proj_ssm")
        gates = matmul_nt(h, wt_in, l, o_g, GATE_COLS, BF16, TM, 2 * TN, "proj_gate")

        y_att = attention_mixer(qkv)
        g2 = jnp.pad(rwkv_g2[l], ((0, RWKV_G_PAD - RWKV_LORA_G), (0, 0)))
        mu = jnp.pad(rwkv_mu[l], (0, RWKV_PAD - RWKV_COLS))
        y_rwkv = rwkv_mixer(u_rwkv, mu, rwkv_w0[l], rwkv_w2[l], rwkv_a0[l], rwkv_a2[l], g2,
                            rwkv_k_k[l], rwkv_k_a[l], rwkv_r_k[l].reshape(-1), rwkv_ln_w[l], rwkv_ln_b[l])
        y_ssm = mamba2_mixer(u_ssm, ssm_conv_w[l], ssm_conv_b[l], ssm_dt_bias[l], ssm_a_log[l],
                             ssm_d[l], ssm_norm_g[l])

        merged = merge_proj(y_att, y_rwkv, y_ssm, p_attn_b, p_rwkv_b, p_ssm_b, gates, TM, TN, l)
        xt = matmul_residual(merged, w_out_b, xt, TM, 2 * TN, "w_out", l)

        h2 = rms_norm(xt, norm_ffn_g[l], BF16)
        act = ffn_up_ws(h2, w_ffn_gate, w_ffn_up, l, 2 * TM, 256)
        xt = matmul_residual(act, w_down_b, xt, TM // 2, TN, "ffn_down", l)
    out = rms_norm(xt, norm_final_g, F32)
    return out.reshape(B, T, D)
```

```python
import functools

import jax
import jax.numpy as jnp
from jax import lax
from jax.experimental import pallas as pl
from jax.experimental.pallas import tpu as pltpu

F32 = jnp.float32
BF16 = jnp.bfloat16

D_MODEL = 4096
DEPTH = 2
NORM_EPS = 1e-6
ATT_GROUPS = ((128, 1), (512, 4), (2048, 16))
ATT_HEADS = 8
ATT_DIM = 128
ATT_WIDTH = 3072
ATT_OUT = 1024
ATT_COLS = 9216
RWKV_WIDTH = 3072
RWKV_HEAD = 64
RWKV_LORA_W = 128
RWKV_LORA_A = 128
RWKV_LORA_G = 480
RWKV_COLS = 9952
RWKV_LN_EPS = 64e-5
SSM_INNER = 3072
SSM_HEADS = 48
SSM_GROUPS = 8
SSM_STATE = 128
SSM_CONV = 4
SSM_CHUNK = 128
SSM_CONV_DIM = 5120
SSM_COLS = 8240
SSM_NORM_EPS = 1e-5
FFN_HIDDEN = 11008
GATE_COLS = 3 * D_MODEL

RWKV_PAD = 10240
RWKV_G_PAD = 512
SSM_PAD = 8704

LANES = 128
RWKV_CHUNK = 64
VMEM_LIMIT = 56 * 1024 * 1024
FFN_UP_VMEM_LIMIT = 61 * 1024 * 1024


def _params(sem, vmem_limit=VMEM_LIMIT):
    return pltpu.CompilerParams(dimension_semantics=sem, vmem_limit_bytes=vmem_limit)


def _bdot(a, b):
    return jnp.dot(a.astype(BF16), b.astype(BF16), preferred_element_type=F32)


def _bdot_nt(a, b):
    return lax.dot_general(a.astype(BF16), b.astype(BF16), (((1,), (1,)), ((), ())),
                           preferred_element_type=F32)


def _bdot_tn(a, b):
    return lax.dot_general(a.astype(BF16), b.astype(BF16), (((0,), (0,)), ((), ())),
                           preferred_element_type=F32)


def _split2(x):
    hi = x.astype(BF16)
    lo = (x - hi.astype(F32)).astype(BF16)
    return hi, lo


def _split3(x):
    hi = x.astype(BF16)
    r1 = x - hi.astype(F32)
    mid = r1.astype(BF16)
    lo = (r1 - mid.astype(F32)).astype(BF16)
    return hi, mid, lo


def _dot_exact_lhs(m_bf16, x):
    hi, mid, lo = _split3(x)
    d = functools.partial(jnp.dot, preferred_element_type=F32)
    return d(m_bf16, hi) + d(m_bf16, mid) + d(m_bf16, lo)


NEG_LOG2E = -1.4426950408889634


def _sigmoid(x):
    return 1.0 / (1.0 + jnp.exp2(x * NEG_LOG2E))


def _silu(x):
    return x * _sigmoid(x)


def _softplus(x):
    return jnp.maximum(x, 0.0) + jnp.log(1.0 + jnp.exp2(jnp.abs(x) * NEG_LOG2E))


def _rms_kernel(x_ref, g_ref, o_ref):
    x = x_ref[...]
    ms = jnp.mean(x * x, axis=-1, keepdims=True)
    o_ref[...] = (x * lax.rsqrt(ms + NORM_EPS) * g_ref[...]).astype(o_ref.dtype)


def rms_norm(x, g, out_dtype, tr=512):
    T, D = x.shape
    return pl.pallas_call(
        _rms_kernel,
        out_shape=jax.ShapeDtypeStruct((T, D), out_dtype),
        grid=(T // tr,),
        in_specs=[pl.BlockSpec((tr, D), lambda i: (i, 0)),
                  pl.BlockSpec((1, D), lambda i: (0, 0))],
        out_specs=pl.BlockSpec((tr, D), lambda i: (i, 0)),
        compiler_params=_params(("parallel",)),
        name="rms_norm",
    )(x, g.reshape(1, D))


def _mm_nt_kernel(a_ref, w_ref, o_ref):
    o_ref[...] = lax.dot_general(a_ref[...], w_ref[...], (((1,), (1,)), ((), ())),
                                 preferred_element_type=F32).astype(o_ref.dtype)


def matmul_nt(a, wt, layer, row0, n_out, out_dtype, tm, tn, name):
    M, K = a.shape
    assert row0 % 16 == 0

    def body(a_hbm, w_hbm, o_hbm):
        pltpu.emit_pipeline(
            _mm_nt_kernel,
            grid=(M // tm, n_out // tn),
            in_specs=[pl.BlockSpec((tm, K), lambda i, j: (i, 0)),
                      pl.BlockSpec((tn, K), lambda i, j: (j, 0))],
            out_specs=[pl.BlockSpec((tm, tn), lambda i, j: (i, j))],
        )(a_hbm, w_hbm.at[layer, pl.ds(row0, n_out)], o_hbm)

    return _hbm_call(body, jax.ShapeDtypeStruct((M, n_out), out_dtype), name, 2)(a, wt)


def _ffn_up_ws_kernel(h_ref, wg_ref, wu_ref, o_ref, wgb_ref, wub_ref):
    @pl.when(pl.program_id(1) == 0)
    def _():
        wgb_ref[...] = wg_ref[...].astype(BF16)
        wub_ref[...] = wu_ref[...].astype(BF16)

    h = h_ref[...]
    gate = jnp.dot(h, wgb_ref[...], preferred_element_type=F32)
    up = jnp.dot(h, wub_ref[...], preferred_element_type=F32)
    o_ref[...] = (_silu(gate) * up).astype(o_ref.dtype)


def ffn_up_ws(h, wg_stack, wu_stack, layer, tm, tn):
    M, K = h.shape
    N = wg_stack.shape[2]
    wspec = pl.BlockSpec((None, K, tn), lambda j, i: (layer, 0, j))
    return pl.pallas_call(
        _ffn_up_ws_kernel,
        out_shape=jax.ShapeDtypeStruct((M, N), BF16),
        grid=(N // tn, M // tm),
        in_specs=[pl.BlockSpec((tm, K), lambda j, i: (i, 0)), wspec, wspec],
        out_specs=pl.BlockSpec((tm, tn), lambda j, i: (i, j)),
        scratch_shapes=[pltpu.VMEM((K, tn), BF16)] * 2,
        compiler_params=_params(("arbitrary", "arbitrary"), FFN_UP_VMEM_LIMIT),
        name="ffn_up",
    )(h, wg_stack, wu_stack)


def _mm_res_kernel(a_ref, w_ref, x_ref, o_ref):
    o_ref[...] = x_ref[...] + jnp.dot(a_ref[...], w_ref[...], preferred_element_type=F32)


def _hbm_call(pipeline_body, out_shape, name, n_in):
    any_spec = pl.BlockSpec(memory_space=pl.ANY)
    return pl.pallas_call(
        pipeline_body,
        out_shape=out_shape,
        in_specs=[any_spec] * n_in,
        out_specs=any_spec,
        compiler_params=pltpu.CompilerParams(vmem_limit_bytes=VMEM_LIMIT),
        name=name,
    )


def matmul_residual(a, w, x, tm, tn, name, layer=0):
    M, K = a.shape
    N = w.shape[-1]

    def body(a_hbm, w_hbm, x_hbm, o_hbm):
        pltpu.emit_pipeline(
            _mm_res_kernel,
            grid=(M // tm, N // tn),
            in_specs=[pl.BlockSpec((tm, K), lambda i, j: (i, 0)),
                      pl.BlockSpec((K, tn), lambda i, j: (0, j)),
                      pl.BlockSpec((tm, tn), lambda i, j: (i, j))],
            out_specs=[pl.BlockSpec((tm, tn), lambda i, j: (i, j))],
        )(a_hbm, w_hbm.at[layer], x_hbm, o_hbm)

    return _hbm_call(body, jax.ShapeDtypeStruct((M, N), F32), name, 3)(a, w, x)


def _merge_proj_kernel(ya_ref, yr_ref, ys_ref, pa_ref, pr_ref, ps_ref,
                       ga_ref, gr_ref, gs_ref, o_ref):
    def sig(ref):
        return _sigmoid(ref[...].astype(F32))

    acc = sig(ga_ref) * jnp.dot(ya_ref[...], pa_ref[...], preferred_element_type=F32)
    acc += sig(gr_ref) * jnp.dot(yr_ref[...], pr_ref[...], preferred_element_type=F32)
    acc += sig(gs_ref) * jnp.dot(ys_ref[...], ps_ref[...], preferred_element_type=F32)
    o_ref[...] = acc.astype(o_ref.dtype)


def merge_proj(y_att, y_rwkv, y_ssm, p_att, p_rwkv, p_ssm, gates, tm, tn, layer):
    M = y_att.shape[0]
    N = p_att.shape[-1]
    nb = N // tn
    rows = lambda width: pl.BlockSpec((tm, width), lambda i, j: (i, 0))
    cols = lambda depth: pl.BlockSpec((depth, tn), lambda i, j: (0, j))
    gate = lambda g: pl.BlockSpec((tm, tn), lambda i, j: (i, j + g * nb))

    def body(ya, yr, ys, pa, pr, ps, gts, o_hbm):
        pltpu.emit_pipeline(
            _merge_proj_kernel,
            grid=(M // tm, nb),
            in_specs=[rows(y_att.shape[1]), rows(y_rwkv.shape[1]), rows(y_ssm.shape[1]),
                      cols(p_att.shape[1]), cols(p_rwkv.shape[1]), cols(p_ssm.shape[1]),
                      gate(0), gate(1), gate(2)],
            out_specs=[pl.BlockSpec((tm, tn), lambda i, j: (i, j))],
        )(ya, yr, ys, pa.at[layer], pr.at[layer], ps.at[layer], gts, gts, gts, o_hbm)

    return _hbm_call(body, jax.ShapeDtypeStruct((M, N), BF16), "merge_proj", 7)(
        y_att, y_rwkv, y_ssm, p_att, p_rwkv, p_ssm, gates)


ATT_WIN = 2048


def _rows(start, size, stride):
    return pl.ds(start, size) if stride == 1 else pl.ds(start, size, stride=stride)


def _attn_kernel(*refs):
    n_g = len(ATT_GROUPS)
    ins = refs[:5 * n_g]
    y_ref = refs[5 * n_g]
    o_scr, l_scr = refs[5 * n_g + 1:]
    w = pl.program_id(1)
    blk = ATT_DIM
    qi = lax.broadcasted_iota(jnp.int32, (blk, 2 * blk), 0)
    kj = lax.broadcasted_iota(jnp.int32, (blk, 2 * blk), 1)
    dist = qi + blk - kj
    scale = ATT_DIM ** -0.5

    for g, (window, d) in enumerate(ATT_GROUPS):
        q_ref, k_ref, v_ref, kp_ref, vp_ref = ins[5 * g:5 * g + 5]
        halo = d * blk

        def two_blocks(cur_ref, prev_ref, res, m):
            if m > 0:
                return cur_ref[_rows(res + halo * (m - 1), 2 * blk, d), :].astype(BF16)
            return jnp.concatenate([prev_ref[_rows(res, blk, d), :], cur_ref[_rows(res, blk, d), :]],
                                   axis=0).astype(BF16)

        band = (dist >= 0) & (dist <= window // d)
        band_first = band & jnp.logical_or(w > 0, kj >= blk)
        for res in range(d):
            for m in range(ATT_WIN // halo):
                row0 = res + halo * m
                q = q_ref[_rows(row0, blk, d), :].astype(BF16)
                k = two_blocks(k_ref, kp_ref, res, m)
                v = two_blocks(v_ref, vp_ref, res, m)
                s = lax.dot_general(q, k, (((1,), (1,)), ((), ())), preferred_element_type=F32) * scale
                s = jnp.where(band if m > 0 else band_first, s, -jnp.inf)
                mx = jnp.max(s, axis=-1, keepdims=True)
                p = jnp.exp(s - mx)
                den = jnp.sum(p, axis=-1, keepdims=True)
                o = jnp.dot((p / den).astype(BF16), v, preferred_element_type=F32)
                o_scr[g, _rows(row0, blk, d), :] = o
                l_scr[g, _rows(row0, blk, d), :] = jnp.broadcast_to(mx + jnp.log(den), (blk, blk))

    rc = 256
    for c in range(ATT_WIN // rc):
        rows = slice(c * rc, (c + 1) * rc)
        la, lb, lc = l_scr[0, rows, :], l_scr[1, rows, :], l_scr[2, rows, :]
        mx = jnp.maximum(jnp.maximum(la, lb), lc)
        ea, eb, ec = jnp.exp(la - mx), jnp.exp(lb - mx), jnp.exp(lc - mx)
        y = (ea * o_scr[0, rows, :] + eb * o_scr[1, rows, :] + ec * o_scr[2, rows, :]) / (ea + eb + ec)
        y_ref[rows, :] = y.astype(y_ref.dtype)


def attention_mixer(qkv):
    T = qkv.shape[0]
    blk = ATT_DIM
    n_g = len(ATT_GROUPS)
    sect = n_g * ATT_HEADS
    in_specs = []
    for g, (window, d) in enumerate(ATT_GROUPS):
        per_win = ATT_WIN // (d * blk)
        for s in range(3):
            in_specs.append(pl.BlockSpec((ATT_WIN, blk),
                                         lambda h, w, s=s, g=g: (w, s * sect + g * ATT_HEADS + h)))
        for s in (1, 2):
            in_specs.append(pl.BlockSpec(
                (d * blk, blk),
                lambda h, w, s=s, g=g, per_win=per_win: (jnp.maximum(w * per_win - 1, 0),
                                                         s * sect + g * ATT_HEADS + h)))
    return pl.pallas_call(
        _attn_kernel,
        out_shape=jax.ShapeDtypeStruct((T, ATT_OUT), BF16),
        grid=(ATT_HEADS, T // ATT_WIN),
        in_specs=in_specs,
        out_specs=pl.BlockSpec((ATT_WIN, blk), lambda h, w: (w, h)),
        scratch_shapes=[pltpu.VMEM((n_g, ATT_WIN, blk), F32)] * 2,
        compiler_params=_params(("parallel", "arbitrary")),
        name="attention",
    )(*([qkv] * (5 * n_g)))


def _rwkv_prep_rows(u_ref, mu_ref, w0_ref, w2_ref, a0_ref, a2_ref, g2_ref, kk_ref, ka_ref, rk_ref,
                    r_out, lw_out, k_out, v_out, kkn_out, a_out, g_out, bonus_out, carry_ref):
    tb = u_ref.shape[0]
    W = RWKV_WIDTH
    CW = 2 * LANES

    def mixed(lo, width):
        sl = slice(lo, lo + width)
        u = u_ref[:, sl]
        row = lax.broadcasted_iota(jnp.int32, u.shape, 0)
        u_prev = jnp.where(row == 0, jnp.broadcast_to(carry_ref[0:1, sl], u.shape), pltpu.roll(u, 1, 0))
        return u + (u_prev - u) * mu_ref[:, sl]

    def dot3(a_split, b_ref, sl):
        ah, al = a_split
        return jnp.dot(jnp.concatenate([ah, ah, al], axis=1), b_ref[:, sl], preferred_element_type=F32)

    m0 = lax.broadcasted_iota(jnp.int32, (tb, LANES), 1) < RWKV_HEAD

    def head_sums(x):
        cols = []
        for c in range(CW // LANES):
            xb = x[:, c * LANES:(c + 1) * LANES]
            s0 = jnp.sum(jnp.where(m0, xb, 0.0), axis=-1, keepdims=True)
            s1 = jnp.sum(jnp.where(m0, 0.0, xb), axis=-1, keepdims=True)
            cols.append(jnp.where(m0, s0, s1))
        return jnp.concatenate(cols, axis=1)

    o = 3 * W
    th = _split2(jnp.tanh(mixed(o, RWKV_LORA_W)))
    xa = _split2(mixed(o + RWKV_LORA_W, RWKV_LORA_A))
    sg = _split2(_sigmoid(mixed(o + RWKV_LORA_W + RWKV_LORA_A, RWKV_G_PAD)))
    for cc in range(W // CW):
        sl = slice(cc * CW, (cc + 1) * CW)
        r = mixed(cc * CW, CW)
        k = mixed(W + cc * CW, CW)
        v = mixed(2 * W + cc * CW, CW)
        w_log = -_softplus(-(w0_ref[:, sl] + dot3(th, w2_ref, sl))) - 0.5
        lw_out[:, sl] = -jnp.exp(w_log)
        a = _sigmoid(a0_ref[:, sl] + dot3(xa, a2_ref, sl))
        g_out[:, sl] = dot3(sg, g2_ref, sl).astype(g_out.dtype)
        kk = k * kk_ref[:, sl]
        kk = kk / jnp.maximum(jnp.sqrt(head_sums(kk * kk)), 1e-12)
        k2 = k * (1.0 + (a - 1.0) * ka_ref[:, sl])
        bonus_out[:, sl] = (head_sums(r * k2 * rk_ref[:, sl]) * v).astype(bonus_out.dtype)
        r_out[:, sl] = r.astype(r_out.dtype)
        k_out[:, sl] = k2.astype(k_out.dtype)
        v_out[:, sl] = v.astype(v_out.dtype)
        kkn_out[:, sl] = kk.astype(kkn_out.dtype)
        a_out[:, sl] = a.astype(a_out.dtype)
    carry_ref[0:1, :] = u_ref[tb - 1:tb, :]


def _hi_lo_hi(w):
    hi, lo = _split2(w)
    return jnp.concatenate([hi, lo, hi], axis=0)


def _rwkv_chunk(ins, states, consts):
    tril_l, m0, strict, incl, eye = consts
    P = range(len(ins))
    L = ins[0][0].shape[0]
    n2 = 2 * L

    def stack(x):
        return jnp.concatenate([jnp.where(m0, x, 0.0), jnp.where(m0, 0.0, x)], axis=0)

    c = [_dot_exact_lhs(tril_l, ins[p][1]) for p in P]
    c_last = [c[p][L - 1:L, :] for p in P]
    lhs, rhs, r_s, a_s, v_s, lhs2 = [], [], [], [], [], []
    for p in P:
        r, lw, k, v, kk, a = ins[p]
        n_in = jnp.exp(-c[p])
        to_end = jnp.exp(c_last[p] - c[p])
        a_t = -kk * jnp.exp(c[p] - lw)
        b_raw = kk * a
        b_t = b_raw * n_in
        k_t = k * n_in
        a_s.append(stack(a_t))
        r_s.append(stack(r * jnp.exp(c[p])))
        v_s.append(stack(v))
        lhs.append(jnp.concatenate([a_s[p], r_s[p]], axis=0))
        rhs.append(jnp.concatenate([b_t, b_t, k_t, k_t], axis=0))
        lhs2.append(jnp.concatenate([stack(b_raw * to_end), stack(k * to_end)], axis=0))
    big = [_bdot_nt(lhs[p], rhs[p]) for p in P]
    a_ab = [jnp.where(strict, big[p][0:n2, 0:n2], 0.0) for p in P]
    a_ak = [jnp.where(strict, big[p][0:n2, n2:2 * n2], 0.0) for p in P]
    m_rb = [jnp.where(incl, big[p][n2:2 * n2, 0:n2], 0.0) for p in P]
    m_rk = [jnp.where(incl, big[p][n2:2 * n2, n2:2 * n2], 0.0) for p in P]

    t_inv = [eye + a_ab[p] for p in P]
    pw = [_bdot(a_ab[p], a_ab[p]) for p in P]
    av = [_bdot(a_ak[p], v_s[p]) for p in P]
    steps = (L - 1).bit_length() - 1
    for s in range(steps):
        if s < steps - 1:
            res = [_bdot(jnp.concatenate([t_inv[p], pw[p]], axis=0), pw[p]) for p in P]
            t_inv = [t_inv[p] + res[p][0:n2] for p in P]
            pw = [res[p][n2:2 * n2] for p in P]
        else:
            res = [_bdot(t_inv[p], pw[p]) for p in P]
            t_inv = [t_inv[p] + res[p] for p in P]

    sol = [_bdot(t_inv[p], jnp.concatenate([a_s[p], av[p]], axis=1)) for p in P]
    qy = [_bdot(m_rb[p], sol[p]) for p in P]
    yv = [_bdot(m_rk[p], v_s[p]) for p in P]
    gh = [_bdot_tn(lhs2[p], jnp.concatenate(
        [sol[p], jnp.concatenate([jnp.zeros_like(v_s[p]), v_s[p]], axis=1)], axis=0)) for p in P]
    g_mm = [gh[p][:, 0:LANES] for p in P]
    h_mm = [gh[p][:, LANES:2 * LANES] for p in P]
    outs = []
    for p in P:
        q_s = r_s[p] + qy[p][:, 0:LANES]
        y_s = qy[p][:, LANES:2 * LANES] + yv[p]
        q_h = q_s[0:L] + q_s[L:n2]
        y_in = y_s[0:L] + y_s[L:n2]
        g_mat = g_mm[p] + jnp.where(
            eye > 0, jnp.broadcast_to(jnp.exp(c_last[p]), (LANES, LANES)), 0.0)
        y = _bdot(q_h, states[p]) + y_in
        s_new = _bdot(g_mat, states[p]) + h_mm[p]
        outs.append((y, s_new))
    return outs


def _rwkv_kernel(u_ref, mu_ref, w0_ref, w2_ref, a0_ref, a2_ref, g2_ref, kk_ref, ka_ref, rk_ref, lnw_ref, lnb_ref,
                 y_ref, prep_ref, s_ref, yp_ref, carry_ref):
    c = pl.program_id(0)
    L = u_ref.shape[0]
    n2 = 2 * L
    pairs = RWKV_WIDTH // LANES

    @pl.when(c == 0)
    def _():
        s_ref[...] = jnp.zeros_like(s_ref)
        yp_ref[...] = jnp.zeros_like(yp_ref)
        prep_ref[...] = jnp.zeros_like(prep_ref)
        carry_ref[...] = jnp.zeros_like(carry_ref)

    ti = lax.broadcasted_iota(jnp.int32, (L, L), 0)
    tj = lax.broadcasted_iota(jnp.int32, (L, L), 1)
    tril_l = jnp.where(ti >= tj, 1.0, 0.0).astype(BF16)
    m0 = lax.broadcasted_iota(jnp.int32, (L, LANES), 1) < RWKV_HEAD
    si = lax.broadcasted_iota(jnp.int32, (n2, n2), 0)
    sj = lax.broadcasted_iota(jnp.int32, (n2, n2), 1)
    same = (si // L) == (sj // L)
    strict = same & ((sj % L) < (si % L))
    incl = same & ((sj % L) <= (si % L))
    eye = jnp.where(si == sj, 1.0, 0.0).astype(F32)
    consts = (tril_l, m0, strict, incl, eye)

    def half_sums(x):
        s0 = jnp.sum(jnp.where(m0, x, 0.0), axis=-1, keepdims=True)
        s1 = jnp.sum(jnp.where(m0, 0.0, x), axis=-1, keepdims=True)
        return jnp.where(m0, s0, s1)

    sls = [slice(p * LANES, (p + 1) * LANES) for p in range(pairs)]
    cur = c % 2
    inv_n = 1.0 / RWKV_HEAD
    for p in range(pairs):
        sl = sls[p]
        y = yp_ref[1 - cur, p]
        yc = y - half_sums(y) * inv_n
        var = half_sums(yc * yc) * inv_n
        out = yc * lax.rsqrt(var + RWKV_LN_EPS) * lnw_ref[:, sl] + lnb_ref[:, sl] + prep_ref[cur, 7, :, sl]
        y_ref[:, sl] = (out * prep_ref[cur, 6, :, sl]).astype(y_ref.dtype)

    ins = [tuple(prep_ref[1 - cur, i, :, sl] for i in range(6)) for sl in sls]
    outs = _rwkv_chunk(ins, [s_ref[p] for p in range(pairs)], consts)
    for p in range(pairs):
        yp_ref[cur, p] = outs[p][0]
        s_ref[p] = outs[p][1]

    _rwkv_prep_rows(u_ref, mu_ref, w0_ref, w2_ref, a0_ref, a2_ref, g2_ref, kk_ref, ka_ref, rk_ref,
                    *[prep_ref.at[cur, i] for i in range(8)], carry_ref)


def rwkv_mixer(u, mu, w0, w2, a0, a2, g2, k_k, k_a, r_k, ln_w, ln_b):
    T = u.shape[0]
    W = RWKV_WIDTH
    L = RWKV_CHUNK
    nc = T // L
    w2c, a2c, g2c = _hi_lo_hi(w2), _hi_lo_hi(a2), _hi_lo_hi(g2)
    row = lambda n: pl.BlockSpec((1, n), lambda c: (0, 0))
    full = lambda a: pl.BlockSpec(a.shape, lambda c: (0, 0))
    return pl.pallas_call(
        _rwkv_kernel,
        out_shape=jax.ShapeDtypeStruct((T, W), BF16),
        grid=(nc + 2,),
        in_specs=[pl.BlockSpec((L, RWKV_PAD), lambda c: (jnp.minimum(c, nc - 1), 0)), row(RWKV_PAD), row(W),
                  full(w2c), row(W), full(a2c), full(g2c), row(W), row(W), row(W), row(W), row(W)],
        out_specs=pl.BlockSpec((L, W), lambda c: (jnp.maximum(c - 2, 0), 0)),
        scratch_shapes=[pltpu.VMEM((2, 8, L, W), F32),
                        pltpu.VMEM((W // LANES, LANES, LANES), F32),
                        pltpu.VMEM((2, W // LANES, L, LANES), F32),
                        pltpu.VMEM((8, RWKV_PAD), F32)],
        compiler_params=_params(("arbitrary",)),
        name="rwkv",
    )(u, mu.reshape(1, -1), w0.reshape(1, W), w2c, a0.reshape(1, W), a2c, g2c,
      k_k.reshape(1, W), k_a.reshape(1, W), r_k.reshape(1, W), ln_w.reshape(1, W), ln_b.reshape(1, W))


def _ssd_chunk(c, p_ref, cw_ref, cb_ref, dtb_ref, alog_ref, dsk_ref, ng_ref, o_ref,
               ext_ref, xbc_ref, st_ref, y_scr):
    L = p_ref.shape[0]
    X0 = SSM_INNER
    HP = 2 * 64

    @pl.when(c == 0)
    def _():
        ext_ref[0:8, :] = jnp.zeros((8, SSM_CONV_DIM), F32)
        st_ref[...] = jnp.zeros_like(st_ref)

    ext_ref[8:8 + L, :] = p_ref[:, X0:X0 + SSM_CONV_DIM]
    CW = 4 * LANES
    for cc in range(SSM_CONV_DIM // CW):
        sl = slice(cc * CW, (cc + 1) * CW)
        conv = jnp.broadcast_to(cb_ref[:, sl], (L, CW))
        for kk in range(SSM_CONV):
            conv = conv + cw_ref[kk:kk + 1, sl] * ext_ref[pl.ds(8 - (SSM_CONV - 1) + kk, L), sl]
        xbc_ref[:, sl] = _silu(conv)
    ext_ref[0:8, :] = p_ref[L - 8:L, X0:X0 + SSM_CONV_DIM]

    dt = _softplus(p_ref[:, X0 + SSM_CONV_DIM:X0 + SSM_CONV_DIM + LANES] + dtb_ref[...])
    d_a = dt * (-jnp.exp(alog_ref[...]))
    ti = lax.broadcasted_iota(jnp.int32, (L, L), 0)
    tj = lax.broadcasted_iota(jnp.int32, (L, L), 1)
    causal = ti >= tj
    a_cum = _dot_exact_lhs(jnp.where(causal, 1.0, 0.0).astype(BF16), d_a)
    a_cum_t = a_cum.T
    dt_t = dt.T
    first = lax.broadcasted_iota(jnp.int32, (L, HP), 1) < 64
    first_n = lax.broadcasted_iota(jnp.int32, (SSM_STATE, HP), 1) < 64

    for g in range(SSM_GROUPS):
        b_m = xbc_ref[:,SSM_INNER + g * SSM_STATE:SSM_INNER + (g + 1) * SSM_STATE]
        c_m = xbc_ref[:,SSM_INNER + SSM_GROUPS * SSM_STATE + g * SSM_STATE:
                  SSM_INNER + SSM_GROUPS * SSM_STATE + (g + 1) * SSM_STATE]
        cb = _bdot_nt(c_m, b_m)
        b_t = b_m.T
        for j in range(3):
            pair = g * 3 + j
            xs = xbc_ref[:,pair * HP:(pair + 1) * HP]
            yd, st, ea, cd = [], [], [], []
            for hh in (2 * pair, 2 * pair + 1):
                col = a_cum[:, hh:hh + 1]
                row = a_cum_t[hh:hh + 1, :]
                dtr = dt_t[hh:hh + 1, :]
                last = a_cum[L - 1:L, hh:hh + 1]
                lmat = jnp.exp(jnp.where(causal, col - row, -jnp.inf))
                yd.append(_bdot(cb * lmat * dtr, xs))
                st.append(_bdot(b_t * (jnp.exp(last - row) * dtr), xs))
                ea.append(jnp.exp(col))
                cd.append(jnp.exp(last))
            s_in = st_ref[pair]
            y = jnp.where(first, yd[0], yd[1])
            y = y + _bdot(c_m, s_in) * jnp.where(first, ea[0], ea[1])
            st_ref[pair] = s_in * jnp.where(first_n, cd[0], cd[1]) + jnp.where(first_n, st[0], st[1])
            y_scr[:, pair * HP:(pair + 1) * HP] = y + xs * dsk_ref[:, pair * HP:(pair + 1) * HP]

    gw = SSM_INNER // SSM_GROUPS
    for g in range(SSM_GROUPS):
        sl = slice(g * gw, (g + 1) * gw)
        yg = y_scr[:, sl] * _silu(p_ref[:, sl])
        ms = jnp.mean(yg * yg, axis=-1, keepdims=True)
        o_ref[:, sl] = (yg * lax.rsqrt(ms + SSM_NORM_EPS) * ng_ref[:, sl]).astype(o_ref.dtype)


def _ssd_kernel(*refs):
    _ssd_chunk(pl.program_id(0), *refs)


def mamba2_mixer(u, conv_w, conv_b, dt_bias, a_log, d_skip, norm_g):
    T = u.shape[0]
    L = SSM_CHUNK
    padh = lambda x: jnp.pad(x, (0, LANES - SSM_HEADS)).reshape(1, LANES)
    full = lambda a: pl.BlockSpec(a.shape, lambda i: (0,) * a.ndim)
    args = (u, conv_w, conv_b.reshape(1, -1), padh(dt_bias), padh(a_log),
            jnp.repeat(d_skip, 64).reshape(1, SSM_INNER), norm_g.reshape(1, SSM_INNER))
    return pl.pallas_call(
        _ssd_kernel,
        out_shape=jax.ShapeDtypeStruct((T, SSM_INNER), BF16),
        grid=(T // L,),
        in_specs=[pl.BlockSpec((L, SSM_PAD), lambda i: (i, 0))] + [full(a) for a in args[1:]],
        out_specs=pl.BlockSpec((L, SSM_INNER), lambda i: (i, 0)),
        scratch_shapes=[pltpu.VMEM((L + 8, SSM_CONV_DIM), F32),
                        pltpu.VMEM((L, SSM_CONV_DIM), F32),
                        pltpu.VMEM((SSM_HEADS // 2, SSM_STATE, LANES), F32),
                        pltpu.VMEM((L, SSM_INNER), F32)],
        compiler_params=_params(("arbitrary",)),
        name="ssd",
    )(*args)


def kernel(x, norm_mix_g, w_in, rwkv_mu, rwkv_w0, rwkv_w2, rwkv_a0, rwkv_a2, rwkv_g2, rwkv_k_k, rwkv_k_a, rwkv_r_k, rwkv_ln_w, rwkv_ln_b, ssm_conv_w, ssm_conv_b, ssm_dt_bias, ssm_a_log, ssm_d, ssm_norm_g, p_attn, p_rwkv, p_ssm, w_out, norm_ffn_g, w_ffn_gate, w_ffn_up, w_ffn_down, norm_final_g):
    B, T, D = x.shape
    assert B == 1 and D == D_MODEL
    xt = x.reshape(T, D)
    o_r = ATT_COLS
    o_s = ATT_COLS + RWKV_COLS
    o_g = ATT_COLS + RWKV_COLS + SSM_COLS
    TM, TN = 1024, 512
    p_attn_b, p_rwkv_b, p_ssm_b = p_attn.astype(BF16), p_rwkv.astype(BF16), p_ssm.astype(BF16)
    w_out_b, w_down_b = w_out.astype(BF16), w_ffn_down.astype(BF16)
    wt_in = jnp.swapaxes(w_in, 1, 2).astype(BF16)
    for l in range(DEPTH):
        h = rms_norm(xt, norm_mix_g[l], BF16)
        qkv = matmul_nt(h, wt_in, l, 0, ATT_COLS, F32, TM, 2 * TN, "proj_att")
        u_rwkv = matmul_nt(h, wt_in, l, o_r, RWKV_PAD, F32, TM, 2 * TN, "proj_rwkv")
        u_ssm = matmul_nt(h, wt_in, l, o_s, SSM_PAD, F32, 2 * TM, TN, "proj_ssm")
        gates = matmul_nt(h, wt_in, l, o_g, GATE_COLS, BF16, TM, 2 * TN, "proj_gate")

        y_att = attention_mixer(qkv)
        g2 = jnp.pad(rwkv_g2[l], ((0, RWKV_G_PAD - RWKV_LORA_G), (0, 0)))
        mu = jnp.pad(rwkv_mu[l], (0, RWKV_PAD - RWKV_COLS))
        y_rwkv = rwkv_mixer(u_rwkv, mu, rwkv_w0[l], rwkv_w2[l], rwkv_a0[l], rwkv_a2[l], g2,
                            rwkv_k_k[l], rwkv_k_a[l], rwkv_r_k[l].reshape(-1), rwkv_ln_w[l], rwkv_ln_b[l])
        y_ssm = mamba2_mixer(u_ssm, ssm_conv_w[l], ssm_conv_b[l], ssm_dt_bias[l], ssm_a_log[l],
                             ssm_d[l], ssm_norm_g[l])

        merged = merge_proj(y_att, y_rwkv, y_ssm, p_attn_b, p_rwkv_b, p_ssm_b, gates, TM, TN, l)
        xt = matmul_residual(merged, w_out_b, xt, TM, 2 * TN, "w_out", l)

        h2 = rms_norm(xt, norm_ffn_g[l], BF16)
        act = ffn_up_ws(h2, w_ffn_gate, w_ffn_up, l, 2 * TM, 256)
        xt = matmul_residual(act, w_down_b, xt, TM // 2, TN, "ffn_down", l)
    out = rms_norm(xt, norm_final_g, F32)
    return out.reshape(B, T, D)
```
